```python
import math
import jax, jax.numpy as jnp
from jax import lax
import numpy as np

D_MODEL = 1024
BATCH = 16
SEQ = 2048
DEPTH = 2

N_META = 16
Q_BLOCK = 128
ROPE_THETA = 10000.0
EPS = 1e-6

MLA_NOPE_DIM = 128
MLA_ROPE_DIM = 64
MLA_V_DIM = 128
MLA_QK_DIM = MLA_NOPE_DIM + MLA_ROPE_DIM
MLA_HEADS = (D_MODEL // 2) // MLA_V_DIM
MLA_Q_RANK = D_MODEL // 4
MLA_KV_RANK = D_MODEL // 8

DIFF_HEAD_DIM = 64
DIFF_V_DIM = 2 * DIFF_HEAD_DIM
DIFF_HEADS = (D_MODEL // 2) // DIFF_V_DIM

MLA_OUT = MLA_HEADS * MLA_V_DIM
DIFF_OUT = DIFF_HEADS * DIFF_V_DIM
MIX_WIDTH = MLA_OUT + DIFF_OUT

IN_SIZES = (MLA_Q_RANK, MLA_KV_RANK, MLA_ROPE_DIM,
            DIFF_HEADS * 2 * DIFF_HEAD_DIM, DIFF_HEADS * 2 * DIFF_HEAD_DIM, DIFF_OUT)
IN_WIDTH = sum(IN_SIZES)

D_FF = -(-8 * D_MODEL // (3 * 256)) * 256

kernel_name = 'hymba_mla_diffattn_swiglu'


def rms_norm(x, g):
    xf = x.astype(jnp.float32)
    y = xf * lax.rsqrt(jnp.mean(xf * xf, axis=-1, keepdims=True) + EPS)
    return (y * g.astype(jnp.float32)).astype(x.dtype)


def rope_tables(n, dim):
    pos = jnp.arange(n, dtype=jnp.float32)
    inv = 1.0 / (ROPE_THETA ** (jnp.arange(0, dim, 2, dtype=jnp.float32) / dim))
    ang = pos[:, None] * inv[None, :]
    emb = jnp.concatenate([ang, ang], axis=-1)
    return jnp.cos(emb), jnp.sin(emb)


def apply_rope(x, cos, sin):
    n, dim = cos.shape
    shape = (1, n) + (1,) * (x.ndim - 3) + (dim,)
    c = cos.reshape(shape).astype(x.dtype)
    s = sin.reshape(shape).astype(x.dtype)
    half = dim // 2
    rot = jnp.concatenate([-x[..., half:], x[..., :half]], axis=-1)
    return x * c + rot * s


def query_blocks(n_total):
    bounds = [(0, N_META)]
    for s in range(N_META, n_total, Q_BLOCK):
        bounds.append((s, min(s + Q_BLOCK, n_total)))
    return bounds


def causal_probs(scores, q_start):
    tq, tk = scores.shape[-2], scores.shape[-1]
    q_idx = q_start + jnp.arange(tq)
    k_idx = jnp.arange(tk)
    mask = k_idx[None, :] <= q_idx[:, None]
    scores = jnp.where(mask, scores, jnp.finfo(jnp.float32).min)
    return jax.nn.softmax(scores, axis=-1)


def mla_attention(q, k, v):
    scale = MLA_QK_DIM ** -0.5
    outs = []
    for s, e in query_blocks(q.shape[1]):
        sc = jnp.einsum('bqhd,bkhd->bhqk', q[:, s:e], k[:, :e],
                        preferred_element_type=jnp.float32) * scale
        p = causal_probs(sc, s)
        outs.append(jnp.einsum('bhqk,bkhd->bqhd', p.astype(v.dtype), v[:, :e]))
    return jnp.concatenate(outs, axis=1)


def diff_attention(q, k, v, lam):
    scale = DIFF_HEAD_DIM ** -0.5
    outs = []
    for s, e in query_blocks(q.shape[1]):
        sc = jnp.einsum('bqhcd,bkhcd->bchqk', q[:, s:e], k[:, :e],
                        preferred_element_type=jnp.float32) * scale
        p = causal_probs(sc, s)
        attn = p[:, 0] - lam * p[:, 1]
        outs.append(jnp.einsum('bhqk,bkhd->bqhd', attn.astype(v.dtype), v[:, :e]))
    return jnp.concatenate(outs, axis=1)


def setup_inputs(seed: int = 0) -> dict:
    key = jax.random.key(seed)
    ks = jax.random.split(key, 24)
    f32 = jnp.float32

    def w(k, shape, fan_in):
        return jax.random.normal(k, shape, f32) * (fan_in ** -0.5)

    def gain(k, shape):
        return 1.0 + 0.02 * jax.random.normal(k, shape, f32)

    return {
        'x': jax.random.normal(ks[0], (BATCH, SEQ, D_MODEL), f32),
        'meta_tokens': jax.random.normal(ks[1], (N_META, D_MODEL), f32),
        'attn_norm': gain(ks[2], (DEPTH, D_MODEL)),
        'w_in': w(ks[3], (DEPTH, D_MODEL, IN_WIDTH), D_MODEL),
        'mla_q_a_norm': gain(ks[4], (DEPTH, MLA_Q_RANK)),
        'w_q_up': w(ks[5], (DEPTH, MLA_Q_RANK, MLA_HEADS * MLA_QK_DIM), MLA_Q_RANK),
        'mla_kv_a_norm': gain(ks[6], (DEPTH, MLA_KV_RANK)),
        'w_kv_up': w(ks[7], (DEPTH, MLA_KV_RANK, MLA_HEADS * (MLA_NOPE_DIM + MLA_V_DIM)), MLA_KV_RANK),
        'mla_q_norm': gain(ks[8], (DEPTH, MLA_QK_DIM)),
        'mla_k_norm': gain(ks[9], (DEPTH, MLA_QK_DIM)),
        'diff_q_norm': gain(ks[10], (DEPTH, DIFF_HEAD_DIM)),
        'diff_k_norm': gain(ks[11], (DEPTH, DIFF_HEAD_DIM)),
        'lambda_q1': 0.1 * jax.random.normal(ks[12], (DEPTH, DIFF_HEAD_DIM), f32),
        'lambda_k1': 0.1 * jax.random.normal(ks[13], (DEPTH, DIFF_HEAD_DIM), f32),
        'lambda_q2': 0.1 * jax.random.normal(ks[14], (DEPTH, DIFF_HEAD_DIM), f32),
        'lambda_k2': 0.1 * jax.random.normal(ks[15], (DEPTH, DIFF_HEAD_DIM), f32),
        'diff_subln': gain(ks[16], (DEPTH, DIFF_V_DIM)),
        'w_o': w(ks[17], (DEPTH, MIX_WIDTH, D_MODEL), MIX_WIDTH),
        'ffn_norm': gain(ks[18], (DEPTH, D_MODEL)),
        'w_gate_up': w(ks[19], (DEPTH, D_MODEL, 2 * D_FF), D_MODEL),
        'w_down': w(ks[20], (DEPTH, D_FF, D_MODEL), D_FF),
    }


def reference(x, meta_tokens, attn_norm, w_in, mla_q_a_norm, w_q_up, mla_kv_a_norm,
              w_kv_up, mla_q_norm, mla_k_norm, diff_q_norm, diff_k_norm,
              lambda_q1, lambda_k1, lambda_q2, lambda_k2, diff_subln, w_o,
              ffn_norm, w_gate_up, w_down):
    b = x.shape[0]
    meta = jnp.broadcast_to(meta_tokens[None].astype(x.dtype), (b, N_META, x.shape[2]))
    h_res = jnp.concatenate([meta, x], axis=1)
    n = h_res.shape[1]
    cos_a, sin_a = rope_tables(n, MLA_ROPE_DIM)
    cos_b, sin_b = rope_tables(n, DIFF_HEAD_DIM)
    split_pts = [sum(IN_SIZES[:i + 1]) for i in range(len(IN_SIZES) - 1)]

    for l in range(DEPTH):
        hn = rms_norm(h_res, attn_norm[l])
        proj = hn @ w_in[l]
        cq, ckv, kr, dq, dk, dv = jnp.split(proj, split_pts, axis=-1)

        q = (rms_norm(cq, mla_q_a_norm[l]) @ w_q_up[l]).reshape(b, n, MLA_HEADS, MLA_QK_DIM)
        kv = (rms_norm(ckv, mla_kv_a_norm[l]) @ w_kv_up[l]).reshape(
            b, n, MLA_HEADS, MLA_NOPE_DIM + MLA_V_DIM)
        k_nope, v_a = kv[..., :MLA_NOPE_DIM], kv[..., MLA_NOPE_DIM:]
        k_rope = jnp.broadcast_to(kr[:, :, None, :], (b, n, MLA_HEADS, MLA_ROPE_DIM))
        k = jnp.concatenate([k_nope, k_rope], axis=-1)
        q = rms_norm(q, mla_q_norm[l])
        k = rms_norm(k, mla_k_norm[l])
        q = jnp.concatenate([q[..., :MLA_NOPE_DIM],
                             apply_rope(q[..., MLA_NOPE_DIM:], cos_a, sin_a)], axis=-1)
        k = jnp.concatenate([k[..., :MLA_NOPE_DIM],
                             apply_rope(k[..., MLA_NOPE_DIM:], cos_a, sin_a)], axis=-1)
        o_a = mla_attention(q, k, v_a).reshape(b, n, MLA_OUT)

        qd = rms_norm(dq.reshape(b, n, DIFF_HEADS, 2, DIFF_HEAD_DIM), diff_q_norm[l])
        kd = rms_norm(dk.reshape(b, n, DIFF_HEADS, 2, DIFF_HEAD_DIM), diff_k_norm[l])
        qd = apply_rope(qd, cos_b, sin_b)
        kd = apply_rope(kd, cos_b, sin_b)
        vd = dv.reshape(b, n, DIFF_HEADS, DIFF_V_DIM)
        lam_init = 0.8 - 0.6 * math.exp(-0.3 * l)
        lam = (jnp.exp(jnp.sum(lambda_q1[l].astype(jnp.float32) * lambda_k1[l].astype(jnp.float32)))
               - jnp.exp(jnp.sum(lambda_q2[l].astype(jnp.float32) * lambda_k2[l].astype(jnp.float32)))
               + lam_init)
        o_b = diff_attention(qd, kd, vd, lam)
        o_b = (rms_norm(o_b, diff_subln[l]) * (1.0 - lam_init)).reshape(b, n, DIFF_OUT)

        h_res = h_res + jnp.concatenate([o_a, o_b], axis=-1) @ w_o[l]

        hn = rms_norm(h_res, ffn_norm[l])
        gu = hn @ w_gate_up[l]
        g, u = gu[..., :D_FF], gu[..., D_FF:]
        h_res = h_res + (jax.nn.silu(g) * u) @ w_down[l]

    return h_res[:, N_META:]
```

```python
import functools
import math

import jax
import jax.numpy as jnp
from jax import lax
from jax.experimental import pallas as pl
from jax.experimental.pallas import tpu as pltpu

F32 = jnp.float32
BF16 = jnp.bfloat16

LANES = 128
BF16_ROWS = 16
V7X_VMEM_BYTES = 64 * 1024 * 1024
VMEM_LIMIT_BYTES = V7X_VMEM_BYTES - 8 * 1024 * 1024

N_META = 16
ROPE_THETA = 10000.0
EPS = 1e-6
LOG2E = math.log2(math.e)

NOPE = 128
ROPE = 64
QK_A = NOPE + ROPE
HEAD_PAD = 256
V_DIM = 128
HEADS = 4
DIFF_D = 64

ATT_TILE = 256
ROW_TILE_MAX = 768


def _rms_scale(x):
    return lax.rsqrt(jnp.mean(x * x, axis=-1, keepdims=True) + EPS)


def _rope(x, cos, sin_lo, sin_hi):
    return x * cos + pltpu.roll(x, LANES - ROPE // 2, 1) * sin_lo + pltpu.roll(x, ROPE // 2, 1) * sin_hi


def _inproj_kernel(h_ref, g_attn_ref, w_in_ref, g_qa_ref, w_qup_ref, g_kva_ref, w_kvup_ref,
                   g_q_ref, g_kn_ref, g_kr_ref, g_dq_ref, g_dk_ref,
                   cosa_ref, sa_lo_ref, sa_hi_ref, cosb_ref, sb_lo_ref, sb_hi_ref,
                   qa_ref, ka_ref, va_ref, qd_ref, kd_ref, vd_ref, *, scale_a, scale_d):
    h = h_ref[...]
    hn = h * _rms_scale(h) * g_attn_ref[...]
    proj = jnp.dot(hn.astype(BF16), w_in_ref[...], preferred_element_type=F32)
    cq = proj[:, 0:256]
    ckv = proj[:, 256:384]
    kr = proj[:, 384:512]
    dq = proj[:, 512:1024]
    dk = proj[:, 1024:1536]
    dv = proj[:, 1536:2048]

    cosa, sa_lo, sa_hi = cosa_ref[...], sa_lo_ref[...], sa_hi_ref[...]
    cosb, sb_lo, sb_hi = cosb_ref[...], sb_lo_ref[...], sb_hi_ref[...]

    cqn = cq * _rms_scale(cq) * g_qa_ref[...]
    q = jnp.dot(cqn.astype(BF16), w_qup_ref[...], preferred_element_type=F32)
    g_q = g_q_ref[...]
    for hd in range(HEADS):
        blk = q[:, hd * HEAD_PAD:(hd + 1) * HEAD_PAD]
        inv = lax.rsqrt(jnp.sum(blk * blk, axis=-1, keepdims=True) * (1.0 / QK_A) + EPS)
        qn = blk * inv * g_q
        qa_ref[:, hd * HEAD_PAD:hd * HEAD_PAD + NOPE] = (qn[:, :NOPE] * scale_a).astype(BF16)
        qr = _rope(qn[:, NOPE:], cosa, sa_lo, sa_hi)
        qa_ref[:, hd * HEAD_PAD + NOPE:(hd + 1) * HEAD_PAD] = (qr * scale_a).astype(BF16)

    ckvn = ckv * _rms_scale(ckv) * g_kva_ref[...]
    kv = jnp.dot(ckvn.astype(BF16), w_kvup_ref[...], preferred_element_type=F32)
    kr_ss = jnp.sum(kr * kr, axis=-1, keepdims=True)
    kr_rot = _rope(kr * g_kr_ref[...], cosa, sa_lo, sa_hi)
    g_kn = g_kn_ref[...]
    for hd in range(HEADS):
        kn = kv[:, hd * 2 * NOPE:hd * 2 * NOPE + NOPE]
        inv = lax.rsqrt((jnp.sum(kn * kn, axis=-1, keepdims=True) + kr_ss) * (1.0 / QK_A) + EPS)
        ka_ref[:, hd * HEAD_PAD:hd * HEAD_PAD + NOPE] = (kn * inv * g_kn).astype(BF16)
        ka_ref[:, hd * HEAD_PAD + NOPE:(hd + 1) * HEAD_PAD] = (kr_rot * inv).astype(BF16)
        va_ref[:, hd * V_DIM:(hd + 1) * V_DIM] = kv[:, hd * 2 * NOPE + NOPE:(hd + 1) * 2 * NOPE].astype(BF16)

    lane = lax.broadcasted_iota(jnp.int32, (1, LANES), 1)
    lo = lane < DIFF_D

    def diff_prep(x, g, out_ref, scale):
        for hd in range(HEADS):
            col = x[:, hd * LANES:(hd + 1) * LANES]
            sq = col * col
            ss_lo = jnp.sum(jnp.where(lo, sq, 0.0), axis=-1, keepdims=True)
            ss_hi = jnp.sum(jnp.where(lo, 0.0, sq), axis=-1, keepdims=True)
            inv = jnp.where(lo, lax.rsqrt(ss_lo * (1.0 / DIFF_D) + EPS), lax.rsqrt(ss_hi * (1.0 / DIFF_D) + EPS))
            xn = _rope(col * inv * g, cosb, sb_lo, sb_hi)
            out_ref[:, hd * LANES:(hd + 1) * LANES] = (xn * scale).astype(BF16)

    diff_prep(dq, g_dq_ref[...], qd_ref, scale_d)
    diff_prep(dk, g_dk_ref[...], kd_ref, 1.0)
    vd_ref[...] = dv.astype(BF16)


def _const_spec(shape):
    nd = len(shape)
    return pl.BlockSpec(shape, lambda *_: (0,) * nd, pipeline_mode=pl.Buffered(1))


def _inproj(h, lp, tabs, *, seq_len, row_tile):
    t, d = h.shape
    n_tiles = t // row_tile
    tiles_per_seq = seq_len // row_tile
    row = lambda w: pl.BlockSpec((row_tile, w), lambda i: (i, 0))
    tab = pl.BlockSpec((row_tile, LANES), lambda i: (i % tiles_per_seq, 0))
    consts = [lp['g_attn'], lp['w_in'], lp['g_qa'], lp['w_qup'], lp['g_kva'], lp['w_kvup'],
              lp['g_q'], lp['g_kn'], lp['g_kr'], lp['g_dq'], lp['g_dk']]
    out_w = (HEADS * HEAD_PAD, HEADS * HEAD_PAD, HEADS * V_DIM, HEADS * LANES, HEADS * LANES, HEADS * V_DIM)
    kern = functools.partial(_inproj_kernel, scale_a=QK_A ** -0.5 * LOG2E, scale_d=DIFF_D ** -0.5 * LOG2E)
    return pl.pallas_call(
        kern,
        grid=(n_tiles,),
        in_specs=[row(d)] + [_const_spec(c.shape) for c in consts] + [tab] * 6,
        out_specs=[row(w) for w in out_w],
        out_shape=[jax.ShapeDtypeStruct((t, w), BF16) for w in out_w],
        compiler_params=pltpu.CompilerParams(dimension_semantics=("arbitrary",),
                                             vmem_limit_bytes=VMEM_LIMIT_BYTES),
        name="inproj",
    )(h, *consts, *tabs)


def _nt_dot(a, b):
    return lax.dot_general(a, b, (((1,), (1,)), ((), ())), preferred_element_type=F32)


def _attention_kernel(*refs, n_maps, lam_init, n_tiles):
    if n_maps == 2:
        (q_ref, k_ref, v_ref, lq1_ref, lk1_ref, lq2_ref, lk2_ref, g_sub_ref,
         o_ref, vt_ref, vtm_ref, acc_ref) = refs
        hw = LANES
    else:
        q_ref, k_ref, v_ref, o_ref, vt_ref, vtm_ref, acc_ref = refs
        hw = HEAD_PAD
    n_chains = HEADS * n_maps
    tq = ATT_TILE

    lane = lax.broadcasted_iota(jnp.int32, (1, LANES), 1)

    def q_maps(q):
        if n_maps == 1:
            return [q]
        return [jnp.where(lane < DIFF_D, q, jnp.zeros_like(q)), jnp.where(lane < DIFF_D, jnp.zeros_like(q), q)]

    for hd in range(HEADS):
        vtm_ref[hd] = v_ref[0, 0:N_META, hd * V_DIM:(hd + 1) * V_DIM].astype(F32).T.astype(BF16)
        for c in range(n_tiles):
            r0 = N_META + c * tq
            vt_ref[hd, c] = v_ref[0, r0:r0 + tq, hd * V_DIM:(hd + 1) * V_DIM].astype(F32).T.astype(BF16)

    def finish(o_list):
        outs = []
        if n_maps == 2:
            lam = (jnp.exp(jnp.sum(lq1_ref[...] * lk1_ref[...], axis=-1, keepdims=True))
                   - jnp.exp(jnp.sum(lq2_ref[...] * lk2_ref[...], axis=-1, keepdims=True)) + lam_init)
            g_sub = g_sub_ref[...]
        for hd in range(HEADS):
            if n_maps == 2:
                o = o_list[hd][0] - lam * o_list[hd][1]
                o = o * lax.rsqrt(jnp.mean(o * o, axis=0, keepdims=True) + EPS) * g_sub * (1.0 - lam_init)
            else:
                o = o_list[hd][0]
            outs.append(o.T.astype(BF16))
        return outs

    rows = lax.broadcasted_iota(jnp.int32, (N_META, N_META), 0)
    cols = lax.broadcasted_iota(jnp.int32, (N_META, N_META), 1)
    o_list = []
    for hd in range(HEADS):
        k = k_ref[0, 0:N_META, hd * hw:(hd + 1) * hw]
        per_map = []
        for qm in q_maps(q_ref[0, 0:N_META, hd * hw:(hd + 1) * hw]):
            s = _nt_dot(k, qm)
            s = jnp.where(rows <= cols, s, -jnp.inf)
            p = jnp.exp2(s - jnp.max(s, axis=0, keepdims=True))
            pv = jnp.dot(vtm_ref[hd], p.astype(BF16), preferred_element_type=F32)
            per_map.append(pv * (1.0 / jnp.sum(p, axis=0, keepdims=True)))
        o_list.append(per_map)
    for hd, o in enumerate(finish(o_list)):
        o_ref[0, 0:N_META, hd * V_DIM:(hd + 1) * V_DIM] = o

    krow = lax.broadcasted_iota(jnp.int32, (tq, tq), 0)
    qcol = lax.broadcasted_iota(jnp.int32, (tq, tq), 1)
    causal = krow <= qcol

    def q_tile_body(i, _):
        q0 = pl.multiple_of(N_META + i * tq, BF16_ROWS)
        qs = []
        for hd in range(HEADS):
            qs.extend(q_maps(q_ref[0, pl.ds(q0, tq), hd * hw:(hd + 1) * hw]))

        ms, ls = [], []
        for ch in range(n_chains):
            hd = ch // n_maps
            s = _nt_dot(k_ref[0, 0:N_META, hd * hw:(hd + 1) * hw], qs[ch])
            m = jnp.max(s, axis=0, keepdims=True)
            p = jnp.exp2(s - m)
            acc_ref[ch] = jnp.dot(vtm_ref[hd], p.astype(BF16), preferred_element_type=F32)
            ms.append(m)
            ls.append(jnp.sum(p, axis=0, keepdims=True))

        def chunk(c, ms, ls, masked):
            k0 = pl.multiple_of(N_META + c * tq, BF16_ROWS)
            new_ms, new_ls = [], []
            for ch in range(n_chains):
                hd = ch // n_maps
                s = _nt_dot(k_ref[0, pl.ds(k0, tq), hd * hw:(hd + 1) * hw], qs[ch])
                if masked:
                    s = jnp.where(causal, s, -jnp.inf)
                m_new = jnp.maximum(ms[ch], jnp.max(s, axis=0, keepdims=True))
                alpha = jnp.exp2(ms[ch] - m_new)
                p = jnp.exp2(s - m_new)
                pv = jnp.dot(vt_ref[hd, c], p.astype(BF16), preferred_element_type=F32)
                acc_ref[ch] = alpha * acc_ref[ch] + pv
                new_ms.append(m_new)
                new_ls.append(alpha * ls[ch] + jnp.sum(p, axis=0, keepdims=True))
            return new_ms, new_ls

        def full_chunk(c, carry):
            ms, ls = carry
            ms, ls = chunk(c, list(ms), list(ls), masked=False)
            return tuple(ms), tuple(ls)

        ms, ls = lax.fori_loop(0, i, full_chunk, (tuple(ms), tuple(ls)))
        ms, ls = chunk(i, list(ms), list(ls), masked=True)

        o_list = [[acc_ref[hd * n_maps + mp] * (1.0 / ls[hd * n_maps + mp]) for mp in range(n_maps)]
                  for hd in range(HEADS)]
        for hd, o in enumerate(finish(o_list)):
            o_ref[0, pl.ds(q0, tq), hd * V_DIM:(hd + 1) * V_DIM] = o
        return 0

    lax.fori_loop(0, n_tiles, q_tile_body, 0)


def _attention(q, k, v, *, n_maps, lam_init=0.0, extras=()):
    b, seq_len, wq = q.shape
    n_tiles = (seq_len - N_META) // ATT_TILE
    assert N_META + n_tiles * ATT_TILE == seq_len
    n_chains = HEADS * n_maps
    seq = lambda w: pl.BlockSpec((1, seq_len, w), lambda i: (i, 0, 0))
    kern = functools.partial(_attention_kernel, n_maps=n_maps, lam_init=lam_init, n_tiles=n_tiles)
    return pl.pallas_call(
        kern,
        grid=(b,),
        in_specs=[seq(wq), seq(wq), seq(HEADS * V_DIM)] + [_const_spec(e.shape) for e in extras],
        out_specs=seq(HEADS * V_DIM),
        out_shape=jax.ShapeDtypeStruct((b, seq_len, HEADS * V_DIM), BF16),
        scratch_shapes=[pltpu.VMEM((HEADS, n_tiles, V_DIM, ATT_TILE), BF16),
                        pltpu.VMEM((HEADS, V_DIM, N_META), BF16),
                        pltpu.VMEM((n_chains, V_DIM, ATT_TILE), F32)],
        compiler_params=pltpu.CompilerParams(dimension_semantics=("arbitrary",),
                                             vmem_limit_bytes=VMEM_LIMIT_BYTES),
        name="attn_diff" if n_maps == 2 else "attn_mla",
    )(q, k, v, *extras)


def _out_ffn_kernel(h_ref, oa_ref, ob_ref, wo_ref, g_ref, wgu_ref, wd_ref, out_ref, *, d_ff, ff_chunk):
    half = oa_ref.shape[1]
    h = (h_ref[...]
         + jnp.dot(oa_ref[...], wo_ref[0:half, :], preferred_element_type=F32)
         + jnp.dot(ob_ref[...], wo_ref[half:2 * half, :], preferred_element_type=F32))
    hn = (h * _rms_scale(h) * g_ref[...]).astype(BF16)
    for c0 in range(0, d_ff, ff_chunk):
        g = jnp.dot(hn, wgu_ref[:, c0:c0 + ff_chunk], preferred_element_type=F32)
        u = jnp.dot(hn, wgu_ref[:, d_ff + c0:d_ff + c0 + ff_chunk], preferred_element_type=F32)
        a = (g * jax.nn.sigmoid(g) * u).astype(BF16)
        h = h + jnp.dot(a, wd_ref[c0:c0 + ff_chunk, :], preferred_element_type=F32)
    out_ref[...] = h


def _out_ffn(h, oa, ob, lp, *, row_tile):
    t, d = h.shape
    d_ff = lp['w_d'].shape[0]
    ff_chunk = d_ff
    row = lambda w: pl.BlockSpec((row_tile, w), lambda i: (i, 0))
    consts = [lp['w_o'], lp['g_ffn'], lp['w_gu'], lp['w_d']]
    kern = functools.partial(_out_ffn_kernel, d_ff=d_ff, ff_chunk=ff_chunk)
    return pl.pallas_call(
        kern,
        grid=(t // row_tile,),
        in_specs=[row(d), row(oa.shape[1]), row(ob.shape[1])] + [_const_spec(c.shape) for c in consts],
        out_specs=row(d),
        out_shape=jax.ShapeDtypeStruct((t, d), F32),
        compiler_params=pltpu.CompilerParams(dimension_semantics=("arbitrary",),
                                             vmem_limit_bytes=VMEM_LIMIT_BYTES),
        name="out_ffn",
    )(h, oa, ob, *consts)


def _rope_tables(n):
    pos = jnp.arange(n, dtype=F32)
    inv = 1.0 / (ROPE_THETA ** (jnp.arange(0, ROPE, 2, dtype=F32) / ROPE))
    ang = pos[:, None] * inv[None, :]
    emb = jnp.concatenate([ang, ang], axis=-1)
    cos, sin = jnp.cos(emb), jnp.sin(emb)
    first = jnp.arange(ROPE) < ROPE // 2
    sin_lo = jnp.where(first, -sin, 0.0)
    sin_hi = jnp.where(first, 0.0, sin)
    zeros = jnp.zeros_like(cos)
    pad = lambda a: jnp.concatenate([a, zeros], axis=-1)
    dup = lambda a: jnp.concatenate([a, a], axis=-1)
    return (pad(cos), pad(sin_lo), pad(sin_hi), dup(cos), dup(sin_lo), dup(sin_hi))


def _layer_params(l, attn_norm, w_in, mla_q_a_norm, w_q_up, mla_kv_a_norm, w_kv_up, mla_q_norm,
                  mla_k_norm, diff_q_norm, diff_k_norm, diff_subln, w_o, ffn_norm, w_gate_up, w_down):
    d = w_in.shape[1]
    q_rank, kv_rank = mla_q_a_norm.shape[1], mla_kv_a_norm.shape[1]
    wi = w_in[l]
    o_kr = q_rank + kv_rank
    w_in_p = jnp.concatenate([wi[:, :o_kr + ROPE], jnp.zeros((d, LANES - ROPE), F32), wi[:, o_kr + ROPE:]], axis=1)
    wq = w_q_up[l].reshape(q_rank, HEADS, QK_A)
    wq = jnp.concatenate([wq, jnp.zeros((q_rank, HEADS, HEAD_PAD - QK_A), F32)], axis=-1)
    zpad = jnp.zeros((HEAD_PAD - QK_A,), F32)
    row2 = lambda a: a.reshape(1, -1)
    return dict(
        g_attn=row2(attn_norm[l]), w_in=w_in_p.astype(BF16),
        g_qa=row2(mla_q_a_norm[l]), w_qup=wq.reshape(q_rank, HEADS * HEAD_PAD).astype(BF16),
        g_kva=row2(mla_kv_a_norm[l]), w_kvup=w_kv_up[l].astype(BF16),
        g_q=row2(jnp.concatenate([mla_q_norm[l], zpad])),
        g_kn=row2(mla_k_norm[l][:NOPE]),
        g_kr=row2(jnp.concatenate([mla_k_norm[l][NOPE:], zpad])),
        g_dq=row2(jnp.concatenate([diff_q_norm[l]] * 2)), g_dk=row2(jnp.concatenate([diff_k_norm[l]] * 2)),
        g_sub=diff_subln[l].reshape(-1, 1),
        w_o=w_o[l].astype(BF16), g_ffn=row2(ffn_norm[l]),
        w_gu=w_gate_up[l].astype(BF16), w_d=w_down[l].astype(BF16),
    )


def _row_tile(seq_len):
    best = BF16_ROWS
    for cand in range(BF16_ROWS, ROW_TILE_MAX + 1, BF16_ROWS):
        if seq_len % cand == 0:
            best = cand
    return best


def kernel(x, meta_tokens, attn_norm, w_in, mla_q_a_norm, w_q_up, mla_kv_a_norm, w_kv_up, mla_q_norm,
           mla_k_norm, diff_q_norm, diff_k_norm, lambda_q1, lambda_k1, lambda_q2, lambda_k2, diff_subln,
           w_o, ffn_norm, w_gate_up, w_down):
    b, s, d = x.shape
    depth = w_in.shape[0]
    seq_len = N_META + s
    meta = jnp.broadcast_to(meta_tokens[None].astype(x.dtype), (b, N_META, d))
    h = jnp.concatenate([meta, x], axis=1).reshape(b * seq_len, d)
    row_tile = _row_tile(seq_len)
    tabs = _rope_tables(seq_len)

    for l in range(depth):
        lp = _layer_params(l, attn_norm, w_in, mla_q_a_norm, w_q_up, mla_kv_a_norm, w_kv_up, mla_q_norm,
                           mla_k_norm, diff_q_norm, diff_k_norm, diff_subln, w_o, ffn_norm, w_gate_up, w_down)
        qa, ka, va, qd, kd, vd = _inproj(h, lp, tabs, seq_len=seq_len, row_tile=row_tile)
        seq3 = lambda a: a.reshape(b, seq_len, a.shape[1])
        oa = _attention(seq3(qa), seq3(ka), seq3(va), n_maps=1)
        lam_init = 0.8 - 0.6 * math.exp(-0.3 * l)
        lams = [a[l].reshape(1, -1) for a in (lambda_q1, lambda_k1, lambda_q2, lambda_k2)]
        ob = _attention(seq3(qd), seq3(kd), seq3(vd), n_maps=2, lam_init=lam_init, extras=lams + [lp['g_sub']])
        h = _out_ffn(h, oa.reshape(b * seq_len, -1), ob.reshape(b * seq_len, -1), lp, row_tile=row_tile)

    return h.reshape(b, seq_len, d)[:, N_META:]
```

```python
import functools
import math

import jax
import jax.numpy as jnp
from jax import lax
from jax.experimental import pallas as pl
from jax.experimental.pallas import tpu as pltpu

F32 = jnp.float32
BF16 = jnp.bfloat16

LANES = 128
BF16_ROWS = 16
V7X_VMEM_BYTES = 64 * 1024 * 1024
VMEM_LIMIT_BYTES = V7X_VMEM_BYTES - 8 * 1024 * 1024

N_META = 16
ROPE_THETA = 10000.0
EPS = 1e-6
LOG2E = math.log2(math.e)

NOPE = 128
ROPE = 64
QK_A = NOPE + ROPE
HEAD_PAD = 256
V_DIM = 128
HEADS = 4
DIFF_D = 64

ATT_TILE = 256
ROW_TILE_MAX = 768


def _rms_scale(x):
    return lax.rsqrt(jnp.mean(x * x, axis=-1, keepdims=True) + EPS)


def _rope(x, cos, sin_lo, sin_hi):
    return x * cos + pltpu.roll(x, LANES - ROPE // 2, 1) * sin_lo + pltpu.roll(x, ROPE // 2, 1) * sin_hi


def _inproj_kernel(h_ref, g_attn_ref, w_in_ref, g_qa_ref, w_qup_ref, g_kva_ref, w_kvup_ref,
                   g_q_ref, g_kn_ref, g_kr_ref, g_dq_ref, g_dk_ref,
                   cosa_ref, sa_lo_ref, sa_hi_ref, cosb_ref, sb_lo_ref, sb_hi_ref,
                   qa_ref, ka_ref, va_ref, qd_ref, kd_ref, vd_ref, *, scale_a, scale_d):
    h = h_ref[...]
    hn = h * _rms_scale(h) * g_attn_ref[...]
    proj = jnp.dot(hn.astype(BF16), w_in_ref[...], preferred_element_type=F32)
    cq = proj[:, 0:256]
    ckv = proj[:, 256:384]
    kr = proj[:, 384:512]
    dq = proj[:, 512:1024]
    dk = proj[:, 1024:1536]
    dv = proj[:, 1536:2048]

    cosa, sa_lo, sa_hi = cosa_ref[...], sa_lo_ref[...], sa_hi_ref[...]
    cosb, sb_lo, sb_hi = cosb_ref[...], sb_lo_ref[...], sb_hi_ref[...]

    cqn = cq * _rms_scale(cq) * g_qa_ref[...]
    q = jnp.dot(cqn.astype(BF16), w_qup_ref[...], preferred_element_type=F32)
    g_q = g_q_ref[...]
    for hd in range(HEADS):
        blk = q[:, hd * HEAD_PAD:(hd + 1) * HEAD_PAD]
        inv = lax.rsqrt(jnp.sum(blk * blk, axis=-1, keepdims=True) * (1.0 / QK_A) + EPS)
        qn = blk * inv * g_q
        qa_ref[:, hd * HEAD_PAD:hd * HEAD_PAD + NOPE] = (qn[:, :NOPE] * scale_a).astype(BF16)
        qr = _rope(qn[:, NOPE:], cosa, sa_lo, sa_hi)
        qa_ref[:, hd * HEAD_PAD + NOPE:(hd + 1) * HEAD_PAD] = (qr * scale_a).astype(BF16)

    ckvn = ckv * _rms_scale(ckv) * g_kva_ref[...]
    kv = jnp.dot(ckvn.astype(BF16), w_kvup_ref[...], preferred_element_type=F32)
    kr_ss = jnp.sum(kr * kr, axis=-1, keepdims=True)
    kr_rot = _rope(kr * g_kr_ref[...], cosa, sa_lo, sa_hi)
    g_kn = g_kn_ref[...]
    for hd in range(HEADS):
        kn = kv[:, hd * 2 * NOPE:hd * 2 * NOPE + NOPE]
        inv = lax.rsqrt((jnp.sum(kn * kn, axis=-1, keepdims=True) + kr_ss) * (1.0 / QK_A) + EPS)
        ka_ref[:, hd * HEAD_PAD:hd * HEAD_PAD + NOPE] = (kn * inv * g_kn).astype(BF16)
        ka_ref[:, hd * HEAD_PAD + NOPE:(hd + 1) * HEAD_PAD] = (kr_rot * inv).astype(BF16)
        va_ref[:, hd * V_DIM:(hd + 1) * V_DIM] = kv[:, hd * 2 * NOPE + NOPE:(hd + 1) * 2 * NOPE].astype(BF16)

    lane = lax.broadcasted_iota(jnp.int32, (1, LANES), 1)
    lo = lane < DIFF_D

    def diff_prep(x, g, out_ref, scale):
        for hd in range(HEADS):
            col = x[:, hd * LANES:(hd + 1) * LANES]
            sq = col * col
            ss_lo = jnp.sum(jnp.where(lo, sq, 0.0), axis=-1, keepdims=True)
            ss_hi = jnp.sum(jnp.where(lo, 0.0, sq), axis=-1, keepdims=True)
            inv = jnp.where(lo, lax.rsqrt(ss_lo * (1.0 / DIFF_D) + EPS), lax.rsqrt(ss_hi * (1.0 / DIFF_D) + EPS))
            xn = _rope(col * inv * g, cosb, sb_lo, sb_hi)
            out_ref[:, hd * LANES:(hd + 1) * LANES] = (xn * scale).astype(BF16)

    diff_prep(dq, g_dq_ref[...], qd_ref, scale_d)
    diff_prep(dk, g_dk_ref[...], kd_ref, 1.0)
    vd_ref[...] = dv.astype(BF16)


def _const_spec(shape):
    nd = len(shape)
    return pl.BlockSpec(shape, lambda *_: (0,) * nd, pipeline_mode=pl.Buffered(1))


def _inproj(h, lp, tabs, *, seq_len, row_tile):
    t, d = h.shape
    n_tiles = t // row_tile
    tiles_per_seq = seq_len // row_tile
    row = lambda w: pl.BlockSpec((row_tile, w), lambda i: (i, 0))
    tab = pl.BlockSpec((row_tile, LANES), lambda i: (i % tiles_per_seq, 0))
    consts = [lp['g_attn'], lp['w_in'], lp['g_qa'], lp['w_qup'], lp['g_kva'], lp['w_kvup'],
              lp['g_q'], lp['g_kn'], lp['g_kr'], lp['g_dq'], lp['g_dk']]
    out_w = (HEADS * HEAD_PAD, HEADS * HEAD_PAD, HEADS * V_DIM, HEADS * LANES, HEADS * LANES, HEADS * V_DIM)
    kern = functools.partial(_inproj_kernel, scale_a=QK_A ** -0.5 * LOG2E, scale_d=DIFF_D ** -0.5 * LOG2E)
    return pl.pallas_call(
        kern,
        grid=(n_tiles,),
        in_specs=[row(d)] + [_const_spec(c.shape) for c in consts] + [tab] * 6,
        out_specs=[row(w) for w in out_w],
        out_shape=[jax.ShapeDtypeStruct((t, w), BF16) for w in out_w],
        compiler_params=pltpu.CompilerParams(dimension_semantics=("arbitrary",),
                                             vmem_limit_bytes=VMEM_LIMIT_BYTES),
        name="inproj",
    )(h, *consts, *tabs)


def _nt_dot(a, b):
    return lax.dot_general(a, b, (((1,), (1,)), ((), ())), preferred_element_type=F32)


def _software_pipeline(n_items, item_at, advance, stage_a, stage_b, stage_c):
    def step(s_items):
        c_item, b_item, a_item = s_items
        if c_item is not None:
            stage_c(*c_item)
        if b_item is not None:
            stage_b(*b_item)
        if a_item is not None:
            stage_a(*a_item)

    static = lambda s: item_at(s) if 0 <= s < n_items else None
    lo, hi = 2, n_items
    for s in range(0, min(lo, n_items + 2)):
        step((static(s - 2), static(s - 1), static(s)))
    if hi > lo:
        def body(_, carry):
            c_item, b_item, a_item = carry
            step((c_item, b_item, a_item))
            return b_item, a_item, advance(a_item)
        as_traced = lambda it: tuple(jnp.int32(v) for v in it)
        lax.fori_loop(lo, hi, body, (as_traced(item_at(0)), as_traced(item_at(1)), as_traced(item_at(2))))
    for s in range(max(lo, hi), n_items + 2):
        step((static(s - 2), static(s - 1), static(s)))


def _attention_kernel(*refs, n_maps, lam_init, n_tiles):
    if n_maps == 2:
        (q_ref, k_ref, v_ref, lq1_ref, lk1_ref, lq2_ref, lk2_ref, g_sub_ref, o_ref,
         vt_ref, vtm_ref, qt_ref, s_ref, sm_ref, p_ref, pm_ref, acc_ref, m_ref, l_ref, alpha_ref) = refs
        hw = LANES
    else:
        (q_ref, k_ref, v_ref, o_ref,
         vt_ref, vtm_ref, qt_ref, s_ref, sm_ref, p_ref, pm_ref, acc_ref, m_ref, l_ref, alpha_ref) = refs
        hw = HEAD_PAD
    n_chains = HEADS * n_maps
    tq = ATT_TILE

    def head_cols(hd):
        return slice(hd * hw, (hd + 1) * hw)

    def v_cols(hd):
        return slice(hd * V_DIM, (hd + 1) * V_DIM)

    def rows(start):
        if isinstance(start, int):
            return slice(start, start + tq)
        return pl.ds(pl.multiple_of(start, BF16_ROWS), tq)

    drow = lax.broadcasted_iota(jnp.int32, (hw, 1), 0)
    for hd in range(HEADS):
        vtm_ref[hd] = v_ref[0, 0:N_META, v_cols(hd)].astype(F32).T.astype(BF16)
        for c in range(n_tiles):
            r0 = N_META + c * tq
            vt_ref[c, hd] = v_ref[0, r0:r0 + tq, v_cols(hd)].astype(F32).T.astype(BF16)
            qt = q_ref[0, r0:r0 + tq, head_cols(hd)].astype(F32).T
            if n_maps == 1:
                qt_ref[c, hd] = qt.astype(BF16)
            else:
                qt_ref[c, 2 * hd] = jnp.where(drow < DIFF_D, qt, 0.0).astype(BF16)
                qt_ref[c, 2 * hd + 1] = jnp.where(drow < DIFF_D, 0.0, qt).astype(BF16)

    def finish(o_list):
        outs = []
        if n_maps == 2:
            lam = (jnp.exp(jnp.sum(lq1_ref[...] * lk1_ref[...], axis=-1, keepdims=True))
                   - jnp.exp(jnp.sum(lq2_ref[...] * lk2_ref[...], axis=-1, keepdims=True)) + lam_init)
            g_sub = g_sub_ref[...]
        for hd in range(HEADS):
            if n_maps == 2:
                o = o_list[hd][0] - lam * o_list[hd][1]
                o = o * lax.rsqrt(jnp.mean(o * o, axis=0, keepdims=True) + EPS) * g_sub * (1.0 - lam_init)
            else:
                o = o_list[hd][0]
            outs.append(o.T.astype(BF16))
        return outs

    lane = lax.broadcasted_iota(jnp.int32, (1, LANES), 1)
    mrow = lax.broadcasted_iota(jnp.int32, (N_META, N_META), 0)
    mcol = lax.broadcasted_iota(jnp.int32, (N_META, N_META), 1)
    o_list = []
    for hd in range(HEADS):
        k = k_ref[0, 0:N_META, head_cols(hd)]
        q = q_ref[0, 0:N_META, head_cols(hd)]
        q_maps = [q] if n_maps == 1 else [jnp.where(lane < DIFF_D, q, jnp.zeros_like(q)),
                                          jnp.where(lane < DIFF_D, jnp.zeros_like(q), q)]
        per_map = []
        for qm in q_maps:
            s = _nt_dot(k, qm)
            s = jnp.where(mrow <= mcol, s, -jnp.inf)
            p = jnp.exp2(s - jnp.max(s, axis=0, keepdims=True))
            pv = jnp.dot(vtm_ref[hd], p.astype(BF16), preferred_element_type=F32)
            per_map.append(pv * (1.0 / jnp.sum(p, axis=0, keepdims=True)))
        o_list.append(per_map)
    for hd, o in enumerate(finish(o_list)):
        o_ref[0, 0:N_META, v_cols(hd)] = o

    krow = lax.broadcasted_iota(jnp.int32, (tq, tq), 0)
    qcol = lax.broadcasted_iota(jnp.int32, (tq, tq), 1)
    causal = krow <= qcol

    def diag_scores(i):
        for ch in range(n_chains):
            hd = ch // n_maps
            qt = qt_ref[i, ch]
            s = jnp.dot(k_ref[0, rows(N_META + i * tq), head_cols(hd)], qt, preferred_element_type=F32)
            s_ref[ch] = jnp.where(causal, s, -jnp.inf)
            sm_ref[ch] = jnp.dot(k_ref[0, 0:N_META, head_cols(hd)], qt, preferred_element_type=F32)

    def diag_softmax(i):
        for ch in range(n_chains):
            s = s_ref[ch]
            sm = sm_ref[ch]
            m = jnp.maximum(jnp.max(s, axis=0, keepdims=True), jnp.max(sm, axis=0, keepdims=True))
            p = jnp.exp2(s - m)
            pm = jnp.exp2(sm - m)
            p_ref[ch] = p.astype(BF16)
            pm_ref[ch] = pm.astype(BF16)
            m_ref[i, ch] = m
            l_ref[i, ch] = jnp.sum(p, axis=0, keepdims=True) + jnp.sum(pm, axis=0, keepdims=True)

    def diag_values(i):
        for ch in range(n_chains):
            hd = ch // n_maps
            acc_ref[i, ch] = (jnp.dot(vt_ref[i, hd], p_ref[ch], preferred_element_type=F32)
                              + jnp.dot(vtm_ref[hd], pm_ref[ch], preferred_element_type=F32))

    _software_pipeline(n_tiles, lambda s: (s,), lambda it: (it[0] + 1,),
                       diag_scores, diag_softmax, diag_values)

    def full_scores(i, c):
        for ch in range(n_chains):
            hd = ch // n_maps
            s_ref[ch] = jnp.dot(k_ref[0, rows(N_META + c * tq), head_cols(hd)], qt_ref[i, ch],
                                preferred_element_type=F32)

    def full_softmax(i, c):
        for ch in range(n_chains):
            s = s_ref[ch]
            m_old = m_ref[i, ch]
            m_new = jnp.maximum(m_old, jnp.max(s, axis=0, keepdims=True))
            alpha = jnp.exp2(m_old - m_new)
            p = jnp.exp2(s - m_new)
            p_ref[ch] = p.astype(BF16)
            alpha_ref[ch] = alpha
            m_ref[i, ch] = m_new
            l_ref[i, ch] = alpha * l_ref[i, ch] + jnp.sum(p, axis=0, keepdims=True)

    def full_values(i, c):
        for ch in range(n_chains):
            hd = ch // n_maps
            pv = jnp.dot(vt_ref[c, hd], p_ref[ch], preferred_element_type=F32)
            acc_ref[i, ch] = alpha_ref[ch] * acc_ref[i, ch] + pv

    pairs = [(i, c) for i in range(1, n_tiles) for c in range(i)]

    def next_pair(it):
        i, c = it
        wrap = c + 1 == i
        return jnp.where(wrap, i + 1, i), jnp.where(wrap, 0, c + 1)

    _software_pipeline(len(pairs), lambda s: pairs[s], next_pair, full_scores, full_softmax, full_values)

    def finalize(i, _):
        o_list = [[acc_ref[i, hd * n_maps + mp] * (1.0 / l_ref[i, hd * n_maps + mp]) for mp in range(n_maps)]
                  for hd in range(HEADS)]
        for hd, o in enumerate(finish(o_list)):
            o_ref[0, rows(N_META + i * tq), v_cols(hd)] = o
        return 0

    lax.fori_loop(0, n_tiles, finalize, 0)


def _attention(q, k, v, *, n_maps, lam_init=0.0, extras=()):
    b, seq_len, wq = q.shape
    n_tiles = (seq_len - N_META) // ATT_TILE
    assert N_META + n_tiles * ATT_TILE == seq_len
    n_chains = HEADS * n_maps
    hw = wq // HEADS
    tq = ATT_TILE
    seq = lambda w: pl.BlockSpec((1, seq_len, w), lambda i: (i, 0, 0))
    kern = functools.partial(_attention_kernel, n_maps=n_maps, lam_init=lam_init, n_tiles=n_tiles)
    scratch = [
        pltpu.VMEM((n_tiles, HEADS, V_DIM, tq), BF16),
        pltpu.VMEM((HEADS, V_DIM, N_META), BF16),
        pltpu.VMEM((n_tiles, n_chains, hw, tq), BF16),
        pltpu.VMEM((n_chains, tq, tq), F32),
        pltpu.VMEM((n_chains, N_META, tq), F32),
        pltpu.VMEM((n_chains, tq, tq), BF16),
        pltpu.VMEM((n_chains, N_META, tq), BF16),
        pltpu.VMEM((n_tiles, n_chains, V_DIM, tq), F32),
        pltpu.VMEM((n_tiles, n_chains, 1, tq), F32),
        pltpu.VMEM((n_tiles, n_chains, 1, tq), F32),
        pltpu.VMEM((n_chains, 1, tq), F32),
    ]
    return pl.pallas_call(
        kern,
        grid=(b,),
        in_specs=[seq(wq), seq(wq), seq(HEADS * V_DIM)] + [_const_spec(e.shape) for e in extras],
        out_specs=seq(HEADS * V_DIM),
        out_shape=jax.ShapeDtypeStruct((b, seq_len, HEADS * V_DIM), BF16),
        scratch_shapes=scratch,
        compiler_params=pltpu.CompilerParams(dimension_semantics=("arbitrary",),
                                             vmem_limit_bytes=VMEM_LIMIT_BYTES),
        name="attn_diff" if n_maps == 2 else "attn_mla",
    )(q, k, v, *extras)


def _out_ffn_kernel(h_ref, oa_ref, ob_ref, wo_ref, g_ref, wgu_ref, wd_ref, out_ref, *, d_ff):
    half = oa_ref.shape[1]
    h = (h_ref[...]
         + jnp.dot(oa_ref[...], wo_ref[0:half, :], preferred_element_type=F32)
         + jnp.dot(ob_ref[...], wo_ref[half:2 * half, :], preferred_element_type=F32))
    hn = (h * _rms_scale(h) * g_ref[...]).astype(BF16)
    g = jnp.dot(hn, wgu_ref[:, 0:d_ff], preferred_element_type=F32)
    u = jnp.dot(hn, wgu_ref[:, d_ff:2 * d_ff], preferred_element_type=F32)
    a = (g * jax.nn.sigmoid(g) * u).astype(BF16)
    out_ref[...] = h + jnp.dot(a, wd_ref[...], preferred_element_type=F32)


def _out_ffn(h, oa, ob, lp, *, row_tile):
    t, d = h.shape
    d_ff = lp['w_d'].shape[0]
    row = lambda w: pl.BlockSpec((row_tile, w), lambda i: (i, 0))
    consts = [lp['w_o'], lp['g_ffn'], lp['w_gu'], lp['w_d']]
    kern = functools.partial(_out_ffn_kernel, d_ff=d_ff)
    return pl.pallas_call(
        kern,
        grid=(t // row_tile,),
        in_specs=[row(d), row(oa.shape[1]), row(ob.shape[1])] + [_const_spec(c.shape) for c in consts],
        out_specs=row(d),
        out_shape=jax.ShapeDtypeStruct((t, d), F32),
        compiler_params=pltpu.CompilerParams(dimension_semantics=("arbitrary",),
                                             vmem_limit_bytes=VMEM_LIMIT_BYTES),
        name="out_ffn",
    )(h, oa, ob, *consts)


def _rope_tables(n):
    pos = jnp.arange(n, dtype=F32)
    inv = 1.0 / (ROPE_THETA ** (jnp.arange(0, ROPE, 2, dtype=F32) / ROPE))
    ang = pos[:, None] * inv[None, :]
    emb = jnp.concatenate([ang, ang], axis=-1)
    cos, sin = jnp.cos(emb), jnp.sin(emb)
    first = jnp.arange(ROPE) < ROPE // 2
    sin_lo = jnp.where(first, -sin, 0.0)
    sin_hi = jnp.where(first, 0.0, sin)
    zeros = jnp.zeros_like(cos)
    pad = lambda a: jnp.concatenate([a, zeros], axis=-1)
    dup = lambda a: jnp.concatenate([a, a], axis=-1)
    return (pad(cos), pad(sin_lo), pad(sin_hi), dup(cos), dup(sin_lo), dup(sin_hi))


def _layer_params(l, attn_norm, w_in, mla_q_a_norm, w_q_up, mla_kv_a_norm, w_kv_up, mla_q_norm,
                  mla_k_norm, diff_q_norm, diff_k_norm, diff_subln, w_o, ffn_norm, w_gate_up, w_down):
    d = w_in.shape[1]
    q_rank, kv_rank = mla_q_a_norm.shape[1], mla_kv_a_norm.shape[1]
    wi = w_in[l]
    o_kr = q_rank + kv_rank
    w_in_p = jnp.concatenate([wi[:, :o_kr + ROPE], jnp.zeros((d, LANES - ROPE), F32), wi[:, o_kr + ROPE:]], axis=1)
    wq = w_q_up[l].reshape(q_rank, HEADS, QK_A)
    wq = jnp.concatenate([wq, jnp.zeros((q_rank, HEADS, HEAD_PAD - QK_A), F32)], axis=-1)
    zpad = jnp.zeros((HEAD_PAD - QK_A,), F32)
    row2 = lambda a: a.reshape(1, -1)
    return dict(
        g_attn=row2(attn_norm[l]), w_in=w_in_p.astype(BF16),
        g_qa=row2(mla_q_a_norm[l]), w_qup=wq.reshape(q_rank, HEADS * HEAD_PAD).astype(BF16),
        g_kva=row2(mla_kv_a_norm[l]), w_kvup=w_kv_up[l].astype(BF16),
        g_q=row2(jnp.concatenate([mla_q_norm[l], zpad])),
        g_kn=row2(mla_k_norm[l][:NOPE]),
        g_kr=row2(jnp.concatenate([mla_k_norm[l][NOPE:], zpad])),
        g_dq=row2(jnp.concatenate([diff_q_norm[l]] * 2)), g_dk=row2(jnp.concatenate([diff_k_norm[l]] * 2)),
        g_sub=diff_subln[l].reshape(-1, 1),
        w_o=w_o[l].astype(BF16), g_ffn=row2(ffn_norm[l]),
        w_gu=w_gate_up[l].astype(BF16), w_d=w_down[l].astype(BF16),
    )


def _row_tile(seq_len):
    best = BF16_ROWS
    for cand in range(BF16_ROWS, ROW_TILE_MAX + 1, BF16_ROWS):
        if seq_len % cand == 0:
            best = cand
    return best


def kernel(x, meta_tokens, attn_norm, w_in, mla_q_a_norm, w_q_up, mla_kv_a_norm, w_kv_up, mla_q_norm,
           mla_k_norm, diff_q_norm, diff_k_norm, lambda_q1, lambda_k1, lambda_q2, lambda_k2, diff_subln,
           w_o, ffn_norm, w_gate_up, w_down):
    b, s, d = x.shape
    depth = w_in.shape[0]
    seq_len = N_META + s
    meta = jnp.broadcast_to(meta_tokens[None].astype(x.dtype), (b, N_META, d))
    h = jnp.concatenate([meta, x], axis=1).reshape(b * seq_len, d)
    row_tile = _row_tile(seq_len)
    tabs = _rope_tables(seq_len)

    for l in range(depth):
        lp = _layer_params(l, attn_norm, w_in, mla_q_a_norm, w_q_up, mla_kv_a_norm, w_kv_up, mla_q_norm,
                           mla_k_norm, diff_q_norm, diff_k_norm, diff_subln, w_o, ffn_norm, w_gate_up, w_down)
        qa, ka, va, qd, kd, vd = _inproj(h, lp, tabs, seq_len=seq_len, row_tile=row_tile)
        seq3 = lambda a: a.reshape(b, seq_len, a.shape[1])
        oa = _attention(seq3(qa), seq3(ka), seq3(va), n_maps=1)
        lam_init = 0.8 - 0.6 * math.exp(-0.3 * l)
        lams = [a[l].reshape(1, -1) for a in (lambda_q1, lambda_k1, lambda_q2, lambda_k2)]
        ob = _attention(seq3(qd), seq3(kd), seq3(vd), n_maps=2, lam_init=lam_init, extras=lams + [lp['g_sub']])
        h = _out_ffn(h, oa.reshape(b * seq_len, -1), ob.reshape(b * seq_len, -1), lp, row_tile=row_tile)

    return h.reshape(b, seq_len, d)[:, N_META:]
```

```python
import functools
import math

import jax
import jax.numpy as jnp
from jax import lax
from jax.experimental import pallas as pl
from jax.experimental.pallas import tpu as pltpu

F32 = jnp.float32
BF16 = jnp.bfloat16

LANES = 128
BF16_ROWS = 16
V7X_VMEM_BYTES = 64 * 1024 * 1024
VMEM_LIMIT_BYTES = V7X_VMEM_BYTES - 8 * 1024 * 1024

N_META = 16
ROPE_THETA = 10000.0
EPS = 1e-6
LOG2E = math.log2(math.e)

NOPE = 128
ROPE = 64
QK_A = NOPE + ROPE
HEAD_PAD = 256
V_DIM = 128
VT_ROWS = V_DIM + BF16_ROWS
HEADS = 4
DIFF_D = 64

ATT_TILE = 256
CHAINS_PER_STEP = 4
ROW_TILE_MAX = 768


def _rms_scale(x):
    return lax.rsqrt(jnp.mean(x * x, axis=-1, keepdims=True) + EPS)


def _rope(x, cos, sin_lo, sin_hi):
    return x * cos + pltpu.roll(x, LANES - ROPE // 2, 1) * sin_lo + pltpu.roll(x, ROPE // 2, 1) * sin_hi


def _inproj_kernel(h_ref, g_attn_ref, w_in_ref, g_qa_ref, w_qup_ref, g_kva_ref, w_kvup_ref,
                   g_q_ref, g_kn_ref, g_kr_ref, g_dq_ref, g_dk_ref,
                   cosa_ref, sa_lo_ref, sa_hi_ref, cosb_ref, sb_lo_ref, sb_hi_ref,
                   qa_ref, ka_ref, va_ref, qd_ref, kd_ref, vd_ref, *, scale_a, scale_d):
    h = h_ref[...]
    hn = h * _rms_scale(h) * g_attn_ref[...]
    proj = jnp.dot(hn.astype(BF16), w_in_ref[...], preferred_element_type=F32)
    cq = proj[:, 0:256]
    ckv = proj[:, 256:384]
    kr = proj[:, 384:512]
    dq = proj[:, 512:1024]
    dk = proj[:, 1024:1536]
    dv = proj[:, 1536:2048]

    cosa, sa_lo, sa_hi = cosa_ref[...], sa_lo_ref[...], sa_hi_ref[...]
    cosb, sb_lo, sb_hi = cosb_ref[...], sb_lo_ref[...], sb_hi_ref[...]

    cqn = cq * _rms_scale(cq) * g_qa_ref[...]
    q = jnp.dot(cqn.astype(BF16), w_qup_ref[...], preferred_element_type=F32)
    g_q = g_q_ref[...]
    for hd in range(HEADS):
        blk = q[:, hd * HEAD_PAD:(hd + 1) * HEAD_PAD]
        inv = lax.rsqrt(jnp.sum(blk * blk, axis=-1, keepdims=True) * (1.0 / QK_A) + EPS)
        qn = blk * inv * g_q
        qa_ref[:, hd * HEAD_PAD:hd * HEAD_PAD + NOPE] = (qn[:, :NOPE] * scale_a).astype(BF16)
        qr = _rope(qn[:, NOPE:], cosa, sa_lo, sa_hi)
        qa_ref[:, hd * HEAD_PAD + NOPE:(hd + 1) * HEAD_PAD] = (qr * scale_a).astype(BF16)

    ckvn = ckv * _rms_scale(ckv) * g_kva_ref[...]
    kv = jnp.dot(ckvn.astype(BF16), w_kvup_ref[...], preferred_element_type=F32)
    kr_ss = jnp.sum(kr * kr, axis=-1, keepdims=True)
    kr_rot = _rope(kr * g_kr_ref[...], cosa, sa_lo, sa_hi)
    g_kn = g_kn_ref[...]
    for hd in range(HEADS):
        kn = kv[:, hd * 2 * NOPE:hd * 2 * NOPE + NOPE]
        inv = lax.rsqrt((jnp.sum(kn * kn, axis=-1, keepdims=True) + kr_ss) * (1.0 / QK_A) + EPS)
        ka_ref[:, hd * HEAD_PAD:hd * HEAD_PAD + NOPE] = (kn * inv * g_kn).astype(BF16)
        ka_ref[:, hd * HEAD_PAD + NOPE:(hd + 1) * HEAD_PAD] = (kr_rot * inv).astype(BF16)
        va_ref[:, hd * V_DIM:(hd + 1) * V_DIM] = kv[:, hd * 2 * NOPE + NOPE:(hd + 1) * 2 * NOPE].astype(BF16)

    lane = lax.broadcasted_iota(jnp.int32, (1, LANES), 1)
    lo = lane < DIFF_D

    def diff_prep(x, g, out_ref, scale):
        for hd in range(HEADS):
            col = x[:, hd * LANES:(hd + 1) * LANES]
            sq = col * col
            ss_lo = jnp.sum(jnp.where(lo, sq, 0.0), axis=-1, keepdims=True)
            ss_hi = jnp.sum(jnp.where(lo, 0.0, sq), axis=-1, keepdims=True)
            inv = jnp.where(lo, lax.rsqrt(ss_lo * (1.0 / DIFF_D) + EPS), lax.rsqrt(ss_hi * (1.0 / DIFF_D) + EPS))
            xn = _rope(col * inv * g, cosb, sb_lo, sb_hi)
            out_ref[:, hd * LANES:(hd + 1) * LANES] = (xn * scale).astype(BF16)

    diff_prep(dq, g_dq_ref[...], qd_ref, scale_d)
    diff_prep(dk, g_dk_ref[...], kd_ref, 1.0)
    vd_ref[...] = dv.astype(BF16)


def _const_spec(shape):
    nd = len(shape)
    return pl.BlockSpec(shape, lambda *_: (0,) * nd, pipeline_mode=pl.Buffered(1))


def _inproj(h, lp, tabs, *, seq_len, row_tile):
    t, d = h.shape
    n_tiles = t // row_tile
    tiles_per_seq = seq_len // row_tile
    row = lambda w: pl.BlockSpec((row_tile, w), lambda i: (i, 0))
    tab = pl.BlockSpec((row_tile, LANES), lambda i: (i % tiles_per_seq, 0))
    consts = [lp['g_attn'], lp['w_in'], lp['g_qa'], lp['w_qup'], lp['g_kva'], lp['w_kvup'],
              lp['g_q'], lp['g_kn'], lp['g_kr'], lp['g_dq'], lp['g_dk']]
    out_w = (HEADS * HEAD_PAD, HEADS * HEAD_PAD, HEADS * V_DIM, HEADS * LANES, HEADS * LANES, HEADS * V_DIM)
    kern = functools.partial(_inproj_kernel, scale_a=QK_A ** -0.5 * LOG2E, scale_d=DIFF_D ** -0.5 * LOG2E)
    return pl.pallas_call(
        kern,
        grid=(n_tiles,),
        in_specs=[row(d)] + [_const_spec(c.shape) for c in consts] + [tab] * 6,
        out_specs=[row(w) for w in out_w],
        out_shape=[jax.ShapeDtypeStruct((t, w), BF16) for w in out_w],
        compiler_params=pltpu.CompilerParams(dimension_semantics=("arbitrary",),
                                             vmem_limit_bytes=VMEM_LIMIT_BYTES),
        name="inproj",
    )(h, *consts, *tabs)


def _nt_dot(a, b):
    return lax.dot_general(a, b, (((1,), (1,)), ((), ())), preferred_element_type=F32)


def _software_pipeline(n_items, item_at, advance, stage_a, stage_b, stage_c):
    def step(s, c_item, b_item, a_item):
        if a_item is not None:
            stage_a(s % 2, *a_item)
        if c_item is not None:
            stage_c(s % 2, *c_item)
        if b_item is not None:
            stage_b((s - 1) % 2, *b_item)

    static = lambda s: item_at(s) if 0 <= s < n_items else None
    lo = 2
    n_pairs = max(n_items - lo, 0) // 2
    for s in range(0, min(lo, n_items + 2)):
        step(s, static(s - 2), static(s - 1), static(s))
    if n_pairs > 0:
        def body(_, carry):
            c_item, b_item, a_item = carry
            step(lo, c_item, b_item, a_item)
            n_item = advance(a_item)
            step(lo + 1, b_item, a_item, n_item)
            return a_item, n_item, advance(n_item)
        as_traced = lambda it: tuple(jnp.int32(v) for v in it)
        lax.fori_loop(0, n_pairs, body, (as_traced(item_at(0)), as_traced(item_at(1)), as_traced(item_at(2))))
    for s in range(lo + 2 * n_pairs, n_items + 2):
        step(s, static(s - 2), static(s - 1), static(s))


def _attention_kernel(*refs, n_maps, lam_init, n_tiles):
    if n_maps == 2:
        (q_ref, k_ref, v_ref, lq1_ref, lk1_ref, lq2_ref, lk2_ref, g_sub_ref, o_ref,
         vt_ref, vtm_ref, qt_ref, s_ref, sm_ref, cmax_ref, p_ref, pm_ref, alpha_ref,
         acc_ref, m_ref) = refs
        hw = LANES
    else:
        (q_ref, k_ref, v_ref, o_ref,
         vt_ref, vtm_ref, qt_ref, s_ref, sm_ref, cmax_ref, p_ref, pm_ref, alpha_ref,
         acc_ref, m_ref) = refs
        hw = HEAD_PAD
    n_chains = HEADS * n_maps
    tq = ATT_TILE

    def head_cols(hd):
        return slice(hd * hw, (hd + 1) * hw)

    def v_cols(hd):
        return slice(hd * V_DIM, (hd + 1) * V_DIM)

    def rows(start):
        if isinstance(start, int):
            return slice(start, start + tq)
        return pl.ds(pl.multiple_of(start, BF16_ROWS), tq)

    drow = lax.broadcasted_iota(jnp.int32, (hw, 1), 0)
    for hd in range(HEADS):
        vtm_ref[hd, 0:V_DIM] = v_ref[0, 0:N_META, v_cols(hd)].astype(F32).T.astype(BF16)
        vtm_ref[hd, V_DIM:VT_ROWS] = jnp.ones((VT_ROWS - V_DIM, N_META), BF16)
        for c in range(n_tiles):
            r0 = N_META + c * tq
            vt_ref[c, hd, 0:V_DIM] = v_ref[0, r0:r0 + tq, v_cols(hd)].astype(F32).T.astype(BF16)
            vt_ref[c, hd, V_DIM:VT_ROWS] = jnp.ones((VT_ROWS - V_DIM, tq), BF16)
            qt = q_ref[0, r0:r0 + tq, head_cols(hd)].astype(F32).T
            if n_maps == 1:
                qt_ref[c, hd] = qt.astype(BF16)
            else:
                qt_ref[c, 2 * hd] = jnp.where(drow < DIFF_D, qt, 0.0).astype(BF16)
                qt_ref[c, 2 * hd + 1] = jnp.where(drow < DIFF_D, 0.0, qt).astype(BF16)

    def finish(o_list):
        outs = []
        if n_maps == 2:
            lam = (jnp.exp(jnp.sum(lq1_ref[...] * lk1_ref[...], axis=-1, keepdims=True))
                   - jnp.exp(jnp.sum(lq2_ref[...] * lk2_ref[...], axis=-1, keepdims=True)) + lam_init)
            g_sub = g_sub_ref[...]
        for hd in range(HEADS):
            if n_maps == 2:
                o = o_list[hd][0] - lam * o_list[hd][1]
                o = o * lax.rsqrt(jnp.mean(o * o, axis=0, keepdims=True) + EPS) * g_sub * (1.0 - lam_init)
            else:
                o = o_list[hd][0]
            outs.append(o.T.astype(BF16))
        return outs

    lane = lax.broadcasted_iota(jnp.int32, (1, LANES), 1)
    mrow = lax.broadcasted_iota(jnp.int32, (N_META, N_META), 0)
    mcol = lax.broadcasted_iota(jnp.int32, (N_META, N_META), 1)
    o_list = []
    for hd in range(HEADS):
        k = k_ref[0, 0:N_META, head_cols(hd)]
        q = q_ref[0, 0:N_META, head_cols(hd)]
        q_maps = [q] if n_maps == 1 else [jnp.where(lane < DIFF_D, q, jnp.zeros_like(q)),
                                          jnp.where(lane < DIFF_D, jnp.zeros_like(q), q)]
        per_map = []
        for qm in q_maps:
            s = _nt_dot(k, qm)
            s = jnp.where(mrow <= mcol, s, -jnp.inf)
            p = jnp.exp2(s - jnp.max(s, axis=0, keepdims=True))
            pv = jnp.dot(vtm_ref[hd], p.astype(BF16), preferred_element_type=F32)
            per_map.append(pv[0:V_DIM] * (1.0 / pv[V_DIM:V_DIM + 1]))
        o_list.append(per_map)
    for hd, o in enumerate(finish(o_list)):
        o_ref[0, 0:N_META, v_cols(hd)] = o

    krow = lax.broadcasted_iota(jnp.int32, (tq, tq), 0)
    qcol = lax.broadcasted_iota(jnp.int32, (tq, tq), 1)
    causal = krow <= qcol

    pairs = [(i, c) for c in range(n_tiles - 1) for i in range(c + 1, n_tiles)]

    def next_pair(it):
        i, c = it
        wrap = i + 1 == n_tiles
        return jnp.where(wrap, c + 2, i + 1), jnp.where(wrap, c + 1, c)

    for chains in [range(g, g + CHAINS_PER_STEP) for g in range(0, n_chains, CHAINS_PER_STEP)]:
        def diag_scores(slot, i, chains=chains):
            for ch in chains:
                hd = ch // n_maps
                qt = qt_ref[i, ch]
                s = jnp.dot(k_ref[0, rows(N_META + i * tq), head_cols(hd)], qt, preferred_element_type=F32)
                s = jnp.where(causal, s, -jnp.inf)
                sm = jnp.dot(k_ref[0, 0:N_META, head_cols(hd)], qt, preferred_element_type=F32)
                s_ref[slot, ch] = s
                sm_ref[slot, ch] = sm
                cmax_ref[slot, ch] = jnp.maximum(jnp.max(s, axis=0, keepdims=True),
                                                 jnp.max(sm, axis=0, keepdims=True))

        def diag_softmax(slot, i, chains=chains):
            for ch in chains:
                m = cmax_ref[slot, ch]
                p = jnp.exp2(s_ref[slot, ch] - m)
                pm = jnp.exp2(sm_ref[slot, ch] - m)
                p_ref[slot, ch] = p.astype(BF16)
                pm_ref[slot, ch] = pm.astype(BF16)
                m_ref[i, ch] = m

        def diag_values(slot, i, chains=chains):
            for ch in chains:
                hd = ch // n_maps
                acc_ref[i, ch] = (jnp.dot(vt_ref[i, hd], p_ref[slot, ch], preferred_element_type=F32)
                                  + jnp.dot(vtm_ref[hd], pm_ref[slot, ch], preferred_element_type=F32))

        _software_pipeline(n_tiles, lambda s: (s,), lambda it: (it[0] + 1,),
                           diag_scores, diag_softmax, diag_values)

        def full_scores(slot, i, c, chains=chains):
            for ch in chains:
                hd = ch // n_maps
                s = jnp.dot(k_ref[0, rows(N_META + c * tq), head_cols(hd)], qt_ref[i, ch],
                            preferred_element_type=F32)
                s_ref[slot, ch] = s
                cmax_ref[slot, ch] = jnp.max(s, axis=0, keepdims=True)

        def full_softmax(slot, i, c, chains=chains):
            for ch in chains:
                m_old = m_ref[i, ch]
                m_new = jnp.maximum(m_old, cmax_ref[slot, ch])
                alpha = jnp.exp2(m_old - m_new)
                p = jnp.exp2(s_ref[slot, ch] - m_new)
                p_ref[slot, ch] = p.astype(BF16)
                alpha_ref[slot, ch] = alpha
                m_ref[i, ch] = m_new

        def full_values(slot, i, c, chains=chains):
            for ch in chains:
                hd = ch // n_maps
                pv = jnp.dot(vt_ref[c, hd], p_ref[slot, ch], preferred_element_type=F32)
                acc_ref[i, ch] = alpha_ref[slot, ch] * acc_ref[i, ch] + pv

        _software_pipeline(len(pairs), lambda s: pairs[s], next_pair, full_scores, full_softmax, full_values)

    def finalize(i, _):
        o_list = [[acc_ref[i, hd * n_maps + mp, 0:V_DIM] * (1.0 / acc_ref[i, hd * n_maps + mp, V_DIM:V_DIM + 1])
                   for mp in range(n_maps)] for hd in range(HEADS)]
        for hd, o in enumerate(finish(o_list)):
            o_ref[0, rows(N_META + i * tq), v_cols(hd)] = o
        return 0

    lax.fori_loop(0, n_tiles, finalize, 0)


def _attention(q, k, v, *, n_maps, lam_init=0.0, extras=()):
    b, seq_len, wq = q.shape
    n_tiles = (seq_len - N_META) // ATT_TILE
    assert N_META + n_tiles * ATT_TILE == seq_len
    n_chains = HEADS * n_maps
    hw = wq // HEADS
    tq = ATT_TILE
    seq = lambda w: pl.BlockSpec((1, seq_len, w), lambda i: (i, 0, 0))
    kern = functools.partial(_attention_kernel, n_maps=n_maps, lam_init=lam_init, n_tiles=n_tiles)
    scratch = [
        pltpu.VMEM((n_tiles, HEADS, VT_ROWS, tq), BF16),
        pltpu.VMEM((HEADS, VT_ROWS, N_META), BF16),
        pltpu.VMEM((n_tiles, n_chains, hw, tq), BF16),
        pltpu.VMEM((2, n_chains, tq, tq), F32),
        pltpu.VMEM((2, n_chains, N_META, tq), F32),
        pltpu.VMEM((2, n_chains, 1, tq), F32),
        pltpu.VMEM((2, n_chains, tq, tq), BF16),
        pltpu.VMEM((2, n_chains, N_META, tq), BF16),
        pltpu.VMEM((2, n_chains, 1, tq), F32),
        pltpu.VMEM((n_tiles, n_chains, VT_ROWS, tq), F32),
        pltpu.VMEM((n_tiles, n_chains, 1, tq), F32),
    ]
    return pl.pallas_call(
        kern,
        grid=(b,),
        in_specs=[seq(wq), seq(wq), seq(HEADS * V_DIM)] + [_const_spec(e.shape) for e in extras],
        out_specs=seq(HEADS * V_DIM),
        out_shape=jax.ShapeDtypeStruct((b, seq_len, HEADS * V_DIM), BF16),
        scratch_shapes=scratch,
        compiler_params=pltpu.CompilerParams(dimension_semantics=("arbitrary",),
                                             vmem_limit_bytes=VMEM_LIMIT_BYTES),
        name="attn_diff" if n_maps == 2 else "attn_mla",
    )(q, k, v, *extras)


def _out_ffn_kernel(h_ref, oa_ref, ob_ref, wo_ref, g_ref, wgu_ref, wd_ref, out_ref, *, d_ff):
    half = oa_ref.shape[1]
    h = (h_ref[...]
         + jnp.dot(oa_ref[...], wo_ref[0:half, :], preferred_element_type=F32)
         + jnp.dot(ob_ref[...], wo_ref[half:2 * half, :], preferred_element_type=F32))
    hn = (h * _rms_scale(h) * g_ref[...]).astype(BF16)
    g = jnp.dot(hn, wgu_ref[:, 0:d_ff], preferred_element_type=F32)
    u = jnp.dot(hn, wgu_ref[:, d_ff:2 * d_ff], preferred_element_type=F32)
    a = (g * jax.nn.sigmoid(g) * u).astype(BF16)
    out_ref[...] = h + jnp.dot(a, wd_ref[...], preferred_element_type=F32)


def _out_ffn(h, oa, ob, lp, *, row_tile):
    t, d = h.shape
    d_ff = lp['w_d'].shape[0]
    row = lambda w: pl.BlockSpec((row_tile, w), lambda i: (i, 0))
    consts = [lp['w_o'], lp['g_ffn'], lp['w_gu'], lp['w_d']]
    kern = functools.partial(_out_ffn_kernel, d_ff=d_ff)
    return pl.pallas_call(
        kern,
        grid=(t // row_tile,),
        in_specs=[row(d), row(oa.shape[1]), row(ob.shape[1])] + [_const_spec(c.shape) for c in consts],
        out_specs=row(d),
        out_shape=jax.ShapeDtypeStruct((t, d), F32),
        compiler_params=pltpu.CompilerParams(dimension_semantics=("arbitrary",),
                                             vmem_limit_bytes=VMEM_LIMIT_BYTES),
        name="out_ffn",
    )(h, oa, ob, *consts)


def _rope_tables(n):
    pos = jnp.arange(n, dtype=F32)
    inv = 1.0 / (ROPE_THETA ** (jnp.arange(0, ROPE, 2, dtype=F32) / ROPE))
    ang = pos[:, None] * inv[None, :]
    emb = jnp.concatenate([ang, ang], axis=-1)
    cos, sin = jnp.cos(emb), jnp.sin(emb)
    first = jnp.arange(ROPE) < ROPE // 2
    sin_lo = jnp.where(first, -sin, 0.0)
    sin_hi = jnp.where(first, 0.0, sin)
    zeros = jnp.zeros_like(cos)
    pad = lambda a: jnp.concatenate([a, zeros], axis=-1)
    dup = lambda a: jnp.concatenate([a, a], axis=-1)
    return (pad(cos), pad(sin_lo), pad(sin_hi), dup(cos), dup(sin_lo), dup(sin_hi))


def _layer_params(l, attn_norm, w_in, mla_q_a_norm, w_q_up, mla_kv_a_norm, w_kv_up, mla_q_norm,
                  mla_k_norm, diff_q_norm, diff_k_norm, diff_subln, w_o, ffn_norm, w_gate_up, w_down):
    d = w_in.shape[1]
    q_rank, kv_rank = mla_q_a_norm.shape[1], mla_kv_a_norm.shape[1]
    wi = w_in[l]
    o_kr = q_rank + kv_rank
    w_in_p = jnp.concatenate([wi[:, :o_kr + ROPE], jnp.zeros((d, LANES - ROPE), F32), wi[:, o_kr + ROPE:]], axis=1)
    wq = w_q_up[l].reshape(q_rank, HEADS, QK_A)
    wq = jnp.concatenate([wq, jnp.zeros((q_rank, HEADS, HEAD_PAD - QK_A), F32)], axis=-1)
    zpad = jnp.zeros((HEAD_PAD - QK_A,), F32)
    row2 = lambda a: a.reshape(1, -1)
    return dict(
        g_attn=row2(attn_norm[l]), w_in=w_in_p.astype(BF16),
        g_qa=row2(mla_q_a_norm[l]), w_qup=wq.reshape(q_rank, HEADS * HEAD_PAD).astype(BF16),
        g_kva=row2(mla_kv_a_norm[l]), w_kvup=w_kv_up[l].astype(BF16),
        g_q=row2(jnp.concatenate([mla_q_norm[l], zpad])),
        g_kn=row2(mla_k_norm[l][:NOPE]),
        g_kr=row2(jnp.concatenate([mla_k_norm[l][NOPE:], zpad])),
        g_dq=row2(jnp.concatenate([diff_q_norm[l]] * 2)), g_dk=row2(jnp.concatenate([diff_k_norm[l]] * 2)),
        g_sub=diff_subln[l].reshape(-1, 1),
        w_o=w_o[l].astype(BF16), g_ffn=row2(ffn_norm[l]),
        w_gu=w_gate_up[l].astype(BF16), w_d=w_down[l].astype(BF16),
    )


def _row_tile(seq_len):
    best = BF16_ROWS
    for cand in range(BF16_ROWS, ROW_TILE_MAX + 1, BF16_ROWS):
        if seq_len % cand == 0:
            best = cand
    return best


def kernel(x, meta_tokens, attn_norm, w_in, mla_q_a_norm, w_q_up, mla_kv_a_norm, w_kv_up, mla_q_norm,
           mla_k_norm, diff_q_norm, diff_k_norm, lambda_q1, lambda_k1, lambda_q2, lambda_k2, diff_subln,
           w_o, ffn_norm, w_gate_up, w_down):
    b, s, d = x.shape
    depth = w_in.shape[0]
    seq_len = N_META + s
    meta = jnp.broadcast_to(meta_tokens[None].astype(x.dtype), (b, N_META, d))
    h = jnp.concatenate([meta, x], axis=1).reshape(b * seq_len, d)
    row_tile = _row_tile(seq_len)
    tabs = _rope_tables(seq_len)

    for l in range(depth):
        lp = _layer_params(l, attn_norm, w_in, mla_q_a_norm, w_q_up, mla_kv_a_norm, w_kv_up, mla_q_norm,
                           mla_k_norm, diff_q_norm, diff_k_norm, diff_subln, w_o, ffn_norm, w_gate_up, w_down)
        qa, ka, va, qd, kd, vd = _inproj(h, lp, tabs, seq_len=seq_len, row_tile=row_tile)
        seq3 = lambda a: a.reshape(b, seq_len, a.shape[1])
        oa = _attention(seq3(qa), seq3(ka), seq3(va), n_maps=1)
        lam_init = 0.8 - 0.6 * math.exp(-0.3 * l)
        lams = [a[l].reshape(1, -1) for a in (lambda_q1, lambda_k1, lambda_q2, lambda_k2)]
        ob = _attention(seq3(qd), seq3(kd), seq3(vd), n_maps=2, lam_init=lam_init, extras=lams + [lp['g_sub']])
        h = _out_ffn(h, oa.reshape(b * seq_len, -1), ob.reshape(b * seq_len, -1), lp, row_tile=row_tile)

    return h.reshape(b, seq_len, d)[:, N_META:]
```

```python
import functools
import math

import jax
import jax.numpy as jnp
from jax import lax
from jax.experimental import pallas as pl
from jax.experimental.pallas import tpu as pltpu

F32 = jnp.float32
BF16 = jnp.bfloat16

LANES = 128
BF16_ROWS = 16
V7X_VMEM_BYTES = 64 * 1024 * 1024
VMEM_LIMIT_BYTES = V7X_VMEM_BYTES - 8 * 1024 * 1024

N_META = 16
ROPE_THETA = 10000.0
EPS = 1e-6
LOG2E = math.log2(math.e)

NOPE = 128
ROPE = 64
QK_A = NOPE + ROPE
HEAD_PAD = 256
V_DIM = 128
VT_ROWS = V_DIM + BF16_ROWS
HEADS = 4
DIFF_D = 64

ATT_TILE = 256
CHAINS_PER_STEP = 4
ROW_TILE_MAX = 768


def _rms_scale(x):
    return lax.rsqrt(jnp.mean(x * x, axis=-1, keepdims=True) + EPS)


def _rope(x, cos, sin_lo, sin_hi):
    return x * cos + pltpu.roll(x, LANES - ROPE // 2, 1) * sin_lo + pltpu.roll(x, ROPE // 2, 1) * sin_hi


def _inproj_kernel(h_ref, g_attn_ref, w_in_ref, g_qa_ref, w_qup_ref, g_kva_ref, w_kvup_ref,
                   g_q_ref, g_kn_ref, g_kr_ref, g_dq_ref, g_dk_ref,
                   cosa_ref, sa_lo_ref, sa_hi_ref, cosb_ref, sb_lo_ref, sb_hi_ref,
                   qa_ref, ka_ref, va_ref, qd_ref, kd_ref, vd_ref, *, scale_a, scale_d):
    h = h_ref[...]
    hn = h * _rms_scale(h) * g_attn_ref[...]
    proj = jnp.dot(hn.astype(BF16), w_in_ref[...], preferred_element_type=F32)
    cq = proj[:, 0:256]
    ckv = proj[:, 256:384]
    kr = proj[:, 384:512]
    dq = proj[:, 512:1024]
    dk = proj[:, 1024:1536]
    dv = proj[:, 1536:2048]

    cosa, sa_lo, sa_hi = cosa_ref[...], sa_lo_ref[...], sa_hi_ref[...]
    cosb, sb_lo, sb_hi = cosb_ref[...], sb_lo_ref[...], sb_hi_ref[...]

    cqn = cq * _rms_scale(cq) * g_qa_ref[...]
    q = jnp.dot(cqn.astype(BF16), w_qup_ref[...], preferred_element_type=F32)
    g_q = g_q_ref[...]
    ones_head = jnp.ones((HEAD_PAD, LANES), BF16)
    ones_lanes = jnp.ones((LANES, LANES), BF16)
    for hd in range(HEADS):
        blk = q[:, hd * HEAD_PAD:(hd + 1) * HEAD_PAD]
        ss = jnp.dot((blk * blk).astype(BF16), ones_head, preferred_element_type=F32)
        inv = lax.rsqrt(ss * (1.0 / QK_A) + EPS)
        qa_ref[:, hd * HEAD_PAD:hd * HEAD_PAD + NOPE] = (blk[:, :NOPE] * inv * g_q[:, :NOPE] * scale_a).astype(BF16)
        qr = _rope(blk[:, NOPE:] * inv * g_q[:, NOPE:], cosa, sa_lo, sa_hi)
        qa_ref[:, hd * HEAD_PAD + NOPE:(hd + 1) * HEAD_PAD] = (qr * scale_a).astype(BF16)

    ckvn = ckv * _rms_scale(ckv) * g_kva_ref[...]
    kv = jnp.dot(ckvn.astype(BF16), w_kvup_ref[...], preferred_element_type=F32)
    kr_ss = jnp.dot((kr * kr).astype(BF16), ones_lanes, preferred_element_type=F32)
    kr_rot = _rope(kr * g_kr_ref[...], cosa, sa_lo, sa_hi)
    g_kn = g_kn_ref[...]
    for hd in range(HEADS):
        kn = kv[:, hd * 2 * NOPE:hd * 2 * NOPE + NOPE]
        kn_ss = jnp.dot((kn * kn).astype(BF16), ones_lanes, preferred_element_type=F32)
        inv = lax.rsqrt((kn_ss + kr_ss) * (1.0 / QK_A) + EPS)
        ka_ref[:, hd * HEAD_PAD:hd * HEAD_PAD + NOPE] = (kn * inv * g_kn).astype(BF16)
        ka_ref[:, hd * HEAD_PAD + NOPE:(hd + 1) * HEAD_PAD] = (kr_rot * inv).astype(BF16)
        va_ref[:, hd * V_DIM:(hd + 1) * V_DIM] = kv[:, hd * 2 * NOPE + NOPE:(hd + 1) * 2 * NOPE].astype(BF16)

    same_map = ((lax.broadcasted_iota(jnp.int32, (LANES, LANES), 0) < DIFF_D)
                == (lax.broadcasted_iota(jnp.int32, (LANES, LANES), 1) < DIFF_D))
    map_ones = jnp.where(same_map, 1.0, 0.0).astype(BF16)

    def diff_prep(x, g, out_ref, scale):
        for hd in range(HEADS):
            col = x[:, hd * LANES:(hd + 1) * LANES]
            ss = jnp.dot((col * col).astype(BF16), map_ones, preferred_element_type=F32)
            inv = lax.rsqrt(ss * (1.0 / DIFF_D) + EPS)
            xn = _rope(col * inv * g, cosb, sb_lo, sb_hi)
            out_ref[:, hd * LANES:(hd + 1) * LANES] = (xn * scale).astype(BF16)

    diff_prep(dq, g_dq_ref[...], qd_ref, scale_d)
    diff_prep(dk, g_dk_ref[...], kd_ref, 1.0)
    vd_ref[...] = dv.astype(BF16)


def _const_spec(shape):
    nd = len(shape)
    return pl.BlockSpec(shape, lambda *_: (0,) * nd, pipeline_mode=pl.Buffered(1))


def _inproj(h, lp, tabs, *, seq_len, row_tile):
    t, d = h.shape
    n_tiles = t // row_tile
    tiles_per_seq = seq_len // row_tile
    row = lambda w: pl.BlockSpec((row_tile, w), lambda i: (i, 0))
    tab = pl.BlockSpec((row_tile, LANES), lambda i: (i % tiles_per_seq, 0))
    consts = [lp['g_attn'], lp['w_in'], lp['g_qa'], lp['w_qup'], lp['g_kva'], lp['w_kvup'],
              lp['g_q'], lp['g_kn'], lp['g_kr'], lp['g_dq'], lp['g_dk']]
    out_w = (HEADS * HEAD_PAD, HEADS * HEAD_PAD, HEADS * V_DIM, HEADS * LANES, HEADS * LANES, HEADS * V_DIM)
    kern = functools.partial(_inproj_kernel, scale_a=QK_A ** -0.5 * LOG2E, scale_d=DIFF_D ** -0.5 * LOG2E)
    return pl.pallas_call(
        kern,
        grid=(n_tiles,),
        in_specs=[row(d)] + [_const_spec(c.shape) for c in consts] + [tab] * 6,
        out_specs=[row(w) for w in out_w],
        out_shape=[jax.ShapeDtypeStruct((t, w), BF16) for w in out_w],
        compiler_params=pltpu.CompilerParams(dimension_semantics=("arbitrary",),
                                             vmem_limit_bytes=VMEM_LIMIT_BYTES),
        name="inproj",
    )(h, *consts, *tabs)


def _nt_dot(a, b):
    return lax.dot_general(a, b, (((1,), (1,)), ((), ())), preferred_element_type=F32)


def _software_pipeline(n_items, item_at, advance, stage_a, stage_b, stage_c):
    def step(s, c_item, b_item, a_item):
        if a_item is not None:
            stage_a(s % 2, *a_item)
        if c_item is not None:
            stage_c(s % 2, *c_item)
        if b_item is not None:
            stage_b((s - 1) % 2, *b_item)

    static = lambda s: item_at(s) if 0 <= s < n_items else None
    lo = 2
    n_pairs = max(n_items - lo, 0) // 2
    for s in range(0, min(lo, n_items + 2)):
        step(s, static(s - 2), static(s - 1), static(s))
    if n_pairs > 0:
        def body(_, carry):
            c_item, b_item, a_item = carry
            step(lo, c_item, b_item, a_item)
            n_item = advance(a_item)
            step(lo + 1, b_item, a_item, n_item)
            return a_item, n_item, advance(n_item)
        as_traced = lambda it: tuple(jnp.int32(v) for v in it)
        lax.fori_loop(0, n_pairs, body, (as_traced(item_at(0)), as_traced(item_at(1)), as_traced(item_at(2))))
    for s in range(lo + 2 * n_pairs, n_items + 2):
        step(s, static(s - 2), static(s - 1), static(s))


def _attention_kernel(*refs, n_maps, lam_init, n_tiles):
    if n_maps == 2:
        (q_ref, k_ref, v_ref, lq1_ref, lk1_ref, lq2_ref, lk2_ref, g_sub_ref, o_ref,
         vt_ref, vtm_ref, qt_ref, s_ref, sm_ref, cmax_ref, p_ref, pm_ref, alpha_ref,
         acc_ref, m_ref) = refs
        hw = LANES
    else:
        (q_ref, k_ref, v_ref, o_ref,
         vt_ref, vtm_ref, qt_ref, s_ref, sm_ref, cmax_ref, p_ref, pm_ref, alpha_ref,
         acc_ref, m_ref) = refs
        hw = HEAD_PAD
    n_chains = HEADS * n_maps
    tq = ATT_TILE

    def head_cols(hd):
        return slice(hd * hw, (hd + 1) * hw)

    def v_cols(hd):
        return slice(hd * V_DIM, (hd + 1) * V_DIM)

    def rows(start):
        if isinstance(start, int):
            return slice(start, start + tq)
        return pl.ds(pl.multiple_of(start, BF16_ROWS), tq)

    drow = lax.broadcasted_iota(jnp.int32, (hw, 1), 0)
    for hd in range(HEADS):
        vtm_ref[hd, 0:V_DIM] = v_ref[0, 0:N_META, v_cols(hd)].astype(F32).T.astype(BF16)
        vtm_ref[hd, V_DIM:VT_ROWS] = jnp.ones((VT_ROWS - V_DIM, N_META), BF16)
        for c in range(n_tiles):
            r0 = N_META + c * tq
            vt_ref[c, hd, 0:V_DIM] = v_ref[0, r0:r0 + tq, v_cols(hd)].astype(F32).T.astype(BF16)
            vt_ref[c, hd, V_DIM:VT_ROWS] = jnp.ones((VT_ROWS - V_DIM, tq), BF16)
            qt = q_ref[0, r0:r0 + tq, head_cols(hd)].astype(F32).T
            if n_maps == 1:
                qt_ref[c, hd] = qt.astype(BF16)
            else:
                qt_ref[c, 2 * hd] = jnp.where(drow < DIFF_D, qt, 0.0).astype(BF16)
                qt_ref[c, 2 * hd + 1] = jnp.where(drow < DIFF_D, 0.0, qt).astype(BF16)

    def finish(o_list):
        outs = []
        if n_maps == 2:
            lam = (jnp.exp(jnp.sum(lq1_ref[...] * lk1_ref[...], axis=-1, keepdims=True))
                   - jnp.exp(jnp.sum(lq2_ref[...] * lk2_ref[...], axis=-1, keepdims=True)) + lam_init)
            g_sub = g_sub_ref[...]
        for hd in range(HEADS):
            if n_maps == 2:
                o = o_list[hd][0] - lam * o_list[hd][1]
                o = o * lax.rsqrt(jnp.mean(o * o, axis=0, keepdims=True) + EPS) * g_sub * (1.0 - lam_init)
            else:
                o = o_list[hd][0]
            outs.append(o.T.astype(BF16))
        return outs

    lane = lax.broadcasted_iota(jnp.int32, (1, LANES), 1)
    mrow = lax.broadcasted_iota(jnp.int32, (N_META, N_META), 0)
    mcol = lax.broadcasted_iota(jnp.int32, (N_META, N_META), 1)
    o_list = []
    for hd in range(HEADS):
        k = k_ref[0, 0:N_META, head_cols(hd)]
        q = q_ref[0, 0:N_META, head_cols(hd)]
        q_maps = [q] if n_maps == 1 else [jnp.where(lane < DIFF_D, q, jnp.zeros_like(q)),
                                          jnp.where(lane < DIFF_D, jnp.zeros_like(q), q)]
        per_map = []
        for qm in q_maps:
            s = _nt_dot(k, qm)
            s = jnp.where(mrow <= mcol, s, -jnp.inf)
            p = jnp.exp2(s - jnp.max(s, axis=0, keepdims=True))
            pv = jnp.dot(vtm_ref[hd], p.astype(BF16), preferred_element_type=F32)
            per_map.append(pv[0:V_DIM] * (1.0 / pv[V_DIM:V_DIM + 1]))
        o_list.append(per_map)
    for hd, o in enumerate(finish(o_list)):
        o_ref[0, 0:N_META, v_cols(hd)] = o

    krow = lax.broadcasted_iota(jnp.int32, (tq, tq), 0)
    qcol = lax.broadcasted_iota(jnp.int32, (tq, tq), 1)
    causal = krow <= qcol

    pairs = [(i, c) for c in range(n_tiles - 1) for i in range(c + 1, n_tiles)]

    def next_pair(it):
        i, c = it
        wrap = i + 1 == n_tiles
        return jnp.where(wrap, c + 2, i + 1), jnp.where(wrap, c + 1, c)

    for chains in [range(g, g + CHAINS_PER_STEP) for g in range(0, n_chains, CHAINS_PER_STEP)]:
        def diag_scores(slot, i, chains=chains):
            for ch in chains:
                hd = ch // n_maps
                qt = qt_ref[i, ch]
                s = jnp.dot(k_ref[0, rows(N_META + i * tq), head_cols(hd)], qt, preferred_element_type=F32)
                s = jnp.where(causal, s, -jnp.inf)
                sm = jnp.dot(k_ref[0, 0:N_META, head_cols(hd)], qt, preferred_element_type=F32)
                s_ref[slot, ch] = s
                sm_ref[slot, ch] = sm
                cmax_ref[slot, ch] = jnp.maximum(jnp.max(s, axis=0, keepdims=True),
                                                 jnp.max(sm, axis=0, keepdims=True))

        def diag_softmax(slot, i, chains=chains):
            for ch in chains:
                m = cmax_ref[slot, ch]
                p = jnp.exp2(s_ref[slot, ch] - m)
                pm = jnp.exp2(sm_ref[slot, ch] - m)
                p_ref[slot, ch] = p.astype(BF16)
                pm_ref[slot, ch] = pm.astype(BF16)
                m_ref[i, ch] = m

        def diag_values(slot, i, chains=chains):
            for ch in chains:
                hd = ch // n_maps
                acc_ref[i, ch] = (jnp.dot(vt_ref[i, hd], p_ref[slot, ch], preferred_element_type=F32)
                                  + jnp.dot(vtm_ref[hd], pm_ref[slot, ch], preferred_element_type=F32))

        _software_pipeline(n_tiles, lambda s: (s,), lambda it: (it[0] + 1,),
                           diag_scores, diag_softmax, diag_values)

        def full_scores(slot, i, c, chains=chains):
            for ch in chains:
                hd = ch // n_maps
                s = jnp.dot(k_ref[0, rows(N_META + c * tq), head_cols(hd)], qt_ref[i, ch],
                            preferred_element_type=F32)
                s_ref[slot, ch] = s
                cmax_ref[slot, ch] = jnp.max(s, axis=0, keepdims=True)

        def full_softmax(slot, i, c, chains=chains):
            for ch in chains:
                m_old = m_ref[i, ch]
                m_new = jnp.maximum(m_old, cmax_ref[slot, ch])
                alpha = jnp.exp2(m_old - m_new)
                p = jnp.exp2(s_ref[slot, ch] - m_new)
                p_ref[slot, ch] = p.astype(BF16)
                alpha_ref[slot, ch] = alpha
                m_ref[i, ch] = m_new

        def full_values(slot, i, c, chains=chains):
            for ch in chains:
                hd = ch // n_maps
                pv = jnp.dot(vt_ref[c, hd], p_ref[slot, ch], preferred_element_type=F32)
                acc_ref[i, ch] = alpha_ref[slot, ch] * acc_ref[i, ch] + pv

        _software_pipeline(len(pairs), lambda s: pairs[s], next_pair, full_scores, full_softmax, full_values)

    def finalize(i, _):
        o_list = [[acc_ref[i, hd * n_maps + mp, 0:V_DIM] * (1.0 / acc_ref[i, hd * n_maps + mp, V_DIM:V_DIM + 1])
                   for mp in range(n_maps)] for hd in range(HEADS)]
        for hd, o in enumerate(finish(o_list)):
            o_ref[0, rows(N_META + i * tq), v_cols(hd)] = o
        return 0

    lax.fori_loop(0, n_tiles, finalize, 0)


def _attention(q, k, v, *, n_maps, lam_init=0.0, extras=()):
    b, seq_len, wq = q.shape
    n_tiles = (seq_len - N_META) // ATT_TILE
    assert N_META + n_tiles * ATT_TILE == seq_len
    n_chains = HEADS * n_maps
    hw = wq // HEADS
    tq = ATT_TILE
    seq = lambda w: pl.BlockSpec((1, seq_len, w), lambda i: (i, 0, 0))
    kern = functools.partial(_attention_kernel, n_maps=n_maps, lam_init=lam_init, n_tiles=n_tiles)
    scratch = [
        pltpu.VMEM((n_tiles, HEADS, VT_ROWS, tq), BF16),
        pltpu.VMEM((HEADS, VT_ROWS, N_META), BF16),
        pltpu.VMEM((n_tiles, n_chains, hw, tq), BF16),
        pltpu.VMEM((2, n_chains, tq, tq), F32),
        pltpu.VMEM((2, n_chains, N_META, tq), F32),
        pltpu.VMEM((2, n_chains, 1, tq), F32),
        pltpu.VMEM((2, n_chains, tq, tq), BF16),
        pltpu.VMEM((2, n_chains, N_META, tq), BF16),
        pltpu.VMEM((2, n_chains, 1, tq), F32),
        pltpu.VMEM((n_tiles, n_chains, VT_ROWS, tq), F32),
        pltpu.VMEM((n_tiles, n_chains, 1, tq), F32),
    ]
    return pl.pallas_call(
        kern,
        grid=(b,),
        in_specs=[seq(wq), seq(wq), seq(HEADS * V_DIM)] + [_const_spec(e.shape) for e in extras],
        out_specs=seq(HEADS * V_DIM),
        out_shape=jax.ShapeDtypeStruct((b, seq_len, HEADS * V_DIM), BF16),
        scratch_shapes=scratch,
        compiler_params=pltpu.CompilerParams(dimension_semantics=("arbitrary",),
                                             vmem_limit_bytes=VMEM_LIMIT_BYTES),
        name="attn_diff" if n_maps == 2 else "attn_mla",
    )(q, k, v, *extras)


def _out_ffn_kernel(h_ref, oa_ref, ob_ref, wo_ref, g_ref, wgu_ref, wd_ref, out_ref, *, d_ff):
    half = oa_ref.shape[1]
    h = (h_ref[...]
         + jnp.dot(oa_ref[...], wo_ref[0:half, :], preferred_element_type=F32)
         + jnp.dot(ob_ref[...], wo_ref[half:2 * half, :], preferred_element_type=F32))
    hn = (h * _rms_scale(h) * g_ref[...]).astype(BF16)
    g = jnp.dot(hn, wgu_ref[:, 0:d_ff], preferred_element_type=F32)
    u = jnp.dot(hn, wgu_ref[:, d_ff:2 * d_ff], preferred_element_type=F32)
    a = (g * jax.nn.sigmoid(g) * u).astype(BF16)
    out_ref[...] = h + jnp.dot(a, wd_ref[...], preferred_element_type=F32)


def _out_ffn(h, oa, ob, lp, *, row_tile):
    t, d = h.shape
    d_ff = lp['w_d'].shape[0]
    row = lambda w: pl.BlockSpec((row_tile, w), lambda i: (i, 0))
    consts = [lp['w_o'], lp['g_ffn'], lp['w_gu'], lp['w_d']]
    kern = functools.partial(_out_ffn_kernel, d_ff=d_ff)
    return pl.pallas_call(
        kern,
        grid=(t // row_tile,),
        in_specs=[row(d), row(oa.shape[1]), row(ob.shape[1])] + [_const_spec(c.shape) for c in consts],
        out_specs=row(d),
        out_shape=jax.ShapeDtypeStruct((t, d), F32),
        compiler_params=pltpu.CompilerParams(dimension_semantics=("arbitrary",),
                                             vmem_limit_bytes=VMEM_LIMIT_BYTES),
        name="out_ffn",
    )(h, oa, ob, *consts)


def _rope_tables(n):
    pos = jnp.arange(n, dtype=F32)
    inv = 1.0 / (ROPE_THETA ** (jnp.arange(0, ROPE, 2, dtype=F32) / ROPE))
    ang = pos[:, None] * inv[None, :]
    emb = jnp.concatenate([ang, ang], axis=-1)
    cos, sin = jnp.cos(emb), jnp.sin(emb)
    first = jnp.arange(ROPE) < ROPE // 2
    sin_lo = jnp.where(first, -sin, 0.0)
    sin_hi = jnp.where(first, 0.0, sin)
    zeros = jnp.zeros_like(cos)
    pad = lambda a: jnp.concatenate([a, zeros], axis=-1)
    dup = lambda a: jnp.concatenate([a, a], axis=-1)
    return (pad(cos), pad(sin_lo), pad(sin_hi), dup(cos), dup(sin_lo), dup(sin_hi))


def _layer_params(l, attn_norm, w_in, mla_q_a_norm, w_q_up, mla_kv_a_norm, w_kv_up, mla_q_norm,
                  mla_k_norm, diff_q_norm, diff_k_norm, diff_subln, w_o, ffn_norm, w_gate_up, w_down):
    d = w_in.shape[1]
    q_rank, kv_rank = mla_q_a_norm.shape[1], mla_kv_a_norm.shape[1]
    wi = w_in[l]
    o_kr = q_rank + kv_rank
    w_in_p = jnp.concatenate([wi[:, :o_kr + ROPE], jnp.zeros((d, LANES - ROPE), F32), wi[:, o_kr + ROPE:]], axis=1)
    wq = w_q_up[l].reshape(q_rank, HEADS, QK_A)
    wq = jnp.concatenate([wq, jnp.zeros((q_rank, HEADS, HEAD_PAD - QK_A), F32)], axis=-1)
    zpad = jnp.zeros((HEAD_PAD - QK_A,), F32)
    row2 = lambda a: a.reshape(1, -1)
    return dict(
        g_attn=row2(attn_norm[l]), w_in=w_in_p.astype(BF16),
        g_qa=row2(mla_q_a_norm[l]), w_qup=wq.reshape(q_rank, HEADS * HEAD_PAD).astype(BF16),
        g_kva=row2(mla_kv_a_norm[l]), w_kvup=w_kv_up[l].astype(BF16),
        g_q=row2(jnp.concatenate([mla_q_norm[l], zpad])),
        g_kn=row2(mla_k_norm[l][:NOPE]),
        g_kr=row2(jnp.concatenate([mla_k_norm[l][NOPE:], zpad])),
        g_dq=row2(jnp.concatenate([diff_q_norm[l]] * 2)), g_dk=row2(jnp.concatenate([diff_k_norm[l]] * 2)),
        g_sub=diff_subln[l].reshape(-1, 1),
        w_o=w_o[l].astype(BF16), g_ffn=row2(ffn_norm[l]),
        w_gu=w_gate_up[l].astype(BF16), w_d=w_down[l].astype(BF16),
    )


def _row_tile(seq_len):
    best = BF16_ROWS
    for cand in range(BF16_ROWS, ROW_TILE_MAX + 1, BF16_ROWS):
        if seq_len % cand == 0:
            best = cand
    return best


def kernel(x, meta_tokens, attn_norm, w_in, mla_q_a_norm, w_q_up, mla_kv_a_norm, w_kv_up, mla_q_norm,
           mla_k_norm, diff_q_norm, diff_k_norm, lambda_q1, lambda_k1, lambda_q2, lambda_k2, diff_subln,
           w_o, ffn_norm, w_gate_up, w_down):
    b, s, d = x.shape
    depth = w_in.shape[0]
    seq_len = N_META + s
    meta = jnp.broadcast_to(meta_tokens[None].astype(x.dtype), (b, N_META, d))
    h = jnp.concatenate([meta, x], axis=1).reshape(b * seq_len, d)
    row_tile = _row_tile(seq_len)
    tabs = _rope_tables(seq_len)

    for l in range(depth):
        lp = _layer_params(l, attn_norm, w_in, mla_q_a_norm, w_q_up, mla_kv_a_norm, w_kv_up, mla_q_norm,
                           mla_k_norm, diff_q_norm, diff_k_norm, diff_subln, w_o, ffn_norm, w_gate_up, w_down)
        qa, ka, va, qd, kd, vd = _inproj(h, lp, tabs, seq_len=seq_len, row_tile=row_tile)
        seq3 = lambda a: a.reshape(b, seq_len, a.shape[1])
        oa = _attention(seq3(qa), seq3(ka), seq3(va), n_maps=1)
        lam_init = 0.8 - 0.6 * math.exp(-0.3 * l)
        lams = [a[l].reshape(1, -1) for a in (lambda_q1, lambda_k1, lambda_q2, lambda_k2)]
        ob = _attention(seq3(qd), seq3(kd), seq3(vd), n_maps=2, lam_init=lam_init, extras=lams + [lp['g_sub']])
        h = _out_ffn(h, oa.reshape(b * seq_len, -1), ob.reshape(b * seq_len, -1), lp, row_tile=row_tile)

    return h.reshape(b, seq_len, d)[:, N_META:]
```

```python
import functools
import math

import jax
import jax.numpy as jnp
from jax import lax
from jax.experimental import pallas as pl
from jax.experimental.pallas import tpu as pltpu

F32 = jnp.float32
BF16 = jnp.bfloat16

LANES = 128
BF16_ROWS = 16
V7X_VMEM_BYTES = 64 * 1024 * 1024
VMEM_LIMIT_BYTES = V7X_VMEM_BYTES - 8 * 1024 * 1024

N_META = 16
ROPE_THETA = 10000.0
EPS = 1e-6
LOG2E = math.log2(math.e)

NOPE = 128
ROPE = 64
QK_A = NOPE + ROPE
HEAD_PAD = 256
V_DIM = 128
VT_ROWS = V_DIM + BF16_ROWS
HEADS = 4
DIFF_D = 64

ATT_TILE = 256
CHAINS_PER_STEP = 4
ROW_TILE_MAX = 768


def _rms_scale(x):
    return lax.rsqrt(jnp.mean(x * x, axis=-1, keepdims=True) + EPS)


def _rope(x, cos, sin_lo, sin_hi):
    return x * cos + pltpu.roll(x, LANES - ROPE // 2, 1) * sin_lo + pltpu.roll(x, ROPE // 2, 1) * sin_hi


def _inproj_kernel(h_ref, g_attn_ref, w_in_ref, g_qa_ref, w_qup_ref, g_kva_ref, w_kvup_ref,
                   g_q_ref, g_kn_ref, g_kr_ref, g_dq_ref, g_dk_ref,
                   cosa_ref, sa_lo_ref, sa_hi_ref, cosb_ref, sb_lo_ref, sb_hi_ref,
                   qa_ref, ka_ref, va_ref, qd_ref, kd_ref, vd_ref, *, scale_a, scale_d):
    h = h_ref[...]
    hn = h * _rms_scale(h) * g_attn_ref[...]
    proj = jnp.dot(hn.astype(BF16), w_in_ref[...], preferred_element_type=F32)
    cq = proj[:, 0:256]
    ckv = proj[:, 256:384]
    kr = proj[:, 384:512]
    dq = proj[:, 512:1024]
    dk = proj[:, 1024:1536]
    dv = proj[:, 1536:2048]

    cosa, sa_lo, sa_hi = cosa_ref[...], sa_lo_ref[...], sa_hi_ref[...]
    cosb, sb_lo, sb_hi = cosb_ref[...], sb_lo_ref[...], sb_hi_ref[...]

    cqn = cq * _rms_scale(cq) * g_qa_ref[...]
    q = jnp.dot(cqn.astype(BF16), w_qup_ref[...], preferred_element_type=F32)
    g_q = g_q_ref[...]
    ones_head = jnp.ones((HEAD_PAD, LANES), BF16)
    ones_lanes = jnp.ones((LANES, LANES), BF16)
    for hd in range(HEADS):
        blk = q[:, hd * HEAD_PAD:(hd + 1) * HEAD_PAD]
        ss = jnp.dot((blk * blk).astype(BF16), ones_head, preferred_element_type=F32)
        inv = lax.rsqrt(ss * (1.0 / QK_A) + EPS)
        qa_ref[:, hd * HEAD_PAD:hd * HEAD_PAD + NOPE] = (blk[:, :NOPE] * inv * g_q[:, :NOPE] * scale_a).astype(BF16)
        qr = _rope(blk[:, NOPE:] * inv * g_q[:, NOPE:], cosa, sa_lo, sa_hi)
        qa_ref[:, hd * HEAD_PAD + NOPE:(hd + 1) * HEAD_PAD] = (qr * scale_a).astype(BF16)

    ckvn = ckv * _rms_scale(ckv) * g_kva_ref[...]
    kv = jnp.dot(ckvn.astype(BF16), w_kvup_ref[...], preferred_element_type=F32)
    kr_ss = jnp.dot((kr * kr).astype(BF16), ones_lanes, preferred_element_type=F32)
    kr_rot = _rope(kr * g_kr_ref[...], cosa, sa_lo, sa_hi)
    g_kn = g_kn_ref[...]
    for hd in range(HEADS):
        kn = kv[:, hd * 2 * NOPE:hd * 2 * NOPE + NOPE]
        kn_ss = jnp.dot((kn * kn).astype(BF16), ones_lanes, preferred_element_type=F32)
        inv = lax.rsqrt((kn_ss + kr_ss) * (1.0 / QK_A) + EPS)
        ka_ref[:, hd * HEAD_PAD:hd * HEAD_PAD + NOPE] = (kn * inv * g_kn).astype(BF16)
        ka_ref[:, hd * HEAD_PAD + NOPE:(hd + 1) * HEAD_PAD] = (kr_rot * inv).astype(BF16)
        va_ref[:, hd * V_DIM:(hd + 1) * V_DIM] = kv[:, hd * 2 * NOPE + NOPE:(hd + 1) * 2 * NOPE].astype(BF16)

    same_map = ((lax.broadcasted_iota(jnp.int32, (LANES, LANES), 0) < DIFF_D)
                == (lax.broadcasted_iota(jnp.int32, (LANES, LANES), 1) < DIFF_D))
    map_ones = jnp.where(same_map, 1.0, 0.0).astype(BF16)

    def diff_prep(x, g, out_ref, scale):
        for hd in range(HEADS):
            col = x[:, hd * LANES:(hd + 1) * LANES]
            ss = jnp.dot((col * col).astype(BF16), map_ones, preferred_element_type=F32)
            inv = lax.rsqrt(ss * (1.0 / DIFF_D) + EPS)
            xn = _rope(col * inv * g, cosb, sb_lo, sb_hi)
            out_ref[:, hd * LANES:(hd + 1) * LANES] = (xn * scale).astype(BF16)

    diff_prep(dq, g_dq_ref[...], qd_ref, scale_d)
    diff_prep(dk, g_dk_ref[...], kd_ref, 1.0)
    vd_ref[...] = dv.astype(BF16)


def _const_spec(shape):
    nd = len(shape)
    return pl.BlockSpec(shape, lambda *_: (0,) * nd, pipeline_mode=pl.Buffered(1))


def _inproj(h, lp, tabs, *, seq_len, row_tile):
    t, d = h.shape
    n_tiles = t // row_tile
    tiles_per_seq = seq_len // row_tile
    row = lambda w: pl.BlockSpec((row_tile, w), lambda i: (i, 0))
    tab = pl.BlockSpec((row_tile, LANES), lambda i: (i % tiles_per_seq, 0))
    consts = [lp['g_attn'], lp['w_in'], lp['g_qa'], lp['w_qup'], lp['g_kva'], lp['w_kvup'],
              lp['g_q'], lp['g_kn'], lp['g_kr'], lp['g_dq'], lp['g_dk']]
    out_w = (HEADS * HEAD_PAD, HEADS * HEAD_PAD, HEADS * V_DIM, HEADS * LANES, HEADS * LANES, HEADS * V_DIM)
    kern = functools.partial(_inproj_kernel, scale_a=QK_A ** -0.5 * LOG2E, scale_d=DIFF_D ** -0.5 * LOG2E)
    return pl.pallas_call(
        kern,
        grid=(n_tiles,),
        in_specs=[row(d)] + [_const_spec(c.shape) for c in consts] + [tab] * 6,
        out_specs=[row(w) for w in out_w],
        out_shape=[jax.ShapeDtypeStruct((t, w), BF16) for w in out_w],
        compiler_params=pltpu.CompilerParams(dimension_semantics=("arbitrary",),
                                             vmem_limit_bytes=VMEM_LIMIT_BYTES),
        name="inproj",
    )(h, *consts, *tabs)


def _nt_dot(a, b):
    return lax.dot_general(a, b, (((1,), (1,)), ((), ())), preferred_element_type=F32)


def _software_pipeline(n_items, item_at, advance, stage_a, stage_b, stage_c):
    def step(s, c_item, b_item, a_item):
        if a_item is not None:
            stage_a(s % 2, *a_item)
        if c_item is not None:
            stage_c(s % 2, *c_item)
        if b_item is not None:
            stage_b((s - 1) % 2, *b_item)

    static = lambda s: item_at(s) if 0 <= s < n_items else None
    lo = 2
    n_pairs = max(n_items - lo, 0) // 2
    for s in range(0, min(lo, n_items + 2)):
        step(s, static(s - 2), static(s - 1), static(s))
    if n_pairs > 0:
        def body(_, carry):
            c_item, b_item, a_item = carry
            step(lo, c_item, b_item, a_item)
            n_item = advance(a_item)
            step(lo + 1, b_item, a_item, n_item)
            return a_item, n_item, advance(n_item)
        as_traced = lambda it: tuple(jnp.int32(v) for v in it)
        lax.fori_loop(0, n_pairs, body, (as_traced(item_at(0)), as_traced(item_at(1)), as_traced(item_at(2))))
    for s in range(lo + 2 * n_pairs, n_items + 2):
        step(s, static(s - 2), static(s - 1), static(s))


def _attention_kernel(*refs, n_maps, lam_init, n_tiles):
    if n_maps == 2:
        (q_ref, k_ref, v_ref, lq1_ref, lk1_ref, lq2_ref, lk2_ref, g_sub_ref, o_ref,
         vt_ref, vtm_ref, qt_ref, s_ref, sm_ref, cmax_ref, p_ref, pm_ref, alpha_ref,
         acc_ref, m_ref) = refs
        hw = LANES
    else:
        (q_ref, k_ref, v_ref, o_ref,
         vt_ref, vtm_ref, qt_ref, s_ref, sm_ref, cmax_ref, p_ref, pm_ref, alpha_ref,
         acc_ref, m_ref) = refs
        hw = HEAD_PAD
    n_chains = HEADS * n_maps
    tq = ATT_TILE

    def head_cols(hd):
        return slice(hd * hw, (hd + 1) * hw)

    def v_cols(hd):
        return slice(hd * V_DIM, (hd + 1) * V_DIM)

    def rows(start):
        if isinstance(start, int):
            return slice(start, start + tq)
        return pl.ds(pl.multiple_of(start, BF16_ROWS), tq)

    drow = lax.broadcasted_iota(jnp.int32, (hw, 1), 0)
    for hd in range(HEADS):
        vtm_ref[hd, 0:V_DIM] = v_ref[0, 0:N_META, v_cols(hd)].astype(F32).T.astype(BF16)
        vtm_ref[hd, V_DIM:VT_ROWS] = jnp.ones((VT_ROWS - V_DIM, N_META), BF16)
        for c in range(n_tiles):
            r0 = N_META + c * tq
            vt_ref[c, hd, 0:V_DIM] = v_ref[0, r0:r0 + tq, v_cols(hd)].astype(F32).T.astype(BF16)
            vt_ref[c, hd, V_DIM:VT_ROWS] = jnp.ones((VT_ROWS - V_DIM, tq), BF16)
            qt = q_ref[0, r0:r0 + tq, head_cols(hd)].astype(F32).T
            if n_maps == 1:
                qt_ref[c, hd] = qt.astype(BF16)
            else:
                qt_ref[c, 2 * hd] = jnp.where(drow < DIFF_D, qt, 0.0).astype(BF16)
                qt_ref[c, 2 * hd + 1] = jnp.where(drow < DIFF_D, 0.0, qt).astype(BF16)

    def finish(o_list):
        outs = []
        if n_maps == 2:
            lam = (jnp.exp(jnp.sum(lq1_ref[...] * lk1_ref[...], axis=-1, keepdims=True))
                   - jnp.exp(jnp.sum(lq2_ref[...] * lk2_ref[...], axis=-1, keepdims=True)) + lam_init)
            g_sub = g_sub_ref[...]
        for hd in range(HEADS):
            if n_maps == 2:
                o = o_list[hd][0] - lam * o_list[hd][1]
                o = o * lax.rsqrt(jnp.mean(o * o, axis=0, keepdims=True) + EPS) * g_sub * (1.0 - lam_init)
            else:
                o = o_list[hd][0]
            outs.append(o.T.astype(BF16))
        return outs

    lane = lax.broadcasted_iota(jnp.int32, (1, LANES), 1)
    mrow = lax.broadcasted_iota(jnp.int32, (N_META, N_META), 0)
    mcol = lax.broadcasted_iota(jnp.int32, (N_META, N_META), 1)
    o_list = []
    for hd in range(HEADS):
        k = k_ref[0, 0:N_META, head_cols(hd)]
        q = q_ref[0, 0:N_META, head_cols(hd)]
        q_maps = [q] if n_maps == 1 else [jnp.where(lane < DIFF_D, q, jnp.zeros_like(q)),
                                          jnp.where(lane < DIFF_D, jnp.zeros_like(q), q)]
        per_map = []
        for qm in q_maps:
            s = _nt_dot(k, qm)
            s = jnp.where(mrow <= mcol, s, -jnp.inf)
            p = jnp.exp2(s - jnp.max(s, axis=0, keepdims=True))
            pv = jnp.dot(vtm_ref[hd], p.astype(BF16), preferred_element_type=F32)
            per_map.append(pv[0:V_DIM] * (1.0 / pv[V_DIM:V_DIM + 1]))
        o_list.append(per_map)
    for hd, o in enumerate(finish(o_list)):
        o_ref[0, 0:N_META, v_cols(hd)] = o

    krow = lax.broadcasted_iota(jnp.int32, (tq, tq), 0)
    qcol = lax.broadcasted_iota(jnp.int32, (tq, tq), 1)
    causal = krow <= qcol

    pairs = [(i, c) for c in range(n_tiles - 1) for i in range(c + 1, n_tiles)]

    def next_pair(it):
        i, c = it
        wrap = i + 1 == n_tiles
        return jnp.where(wrap, c + 2, i + 1), jnp.where(wrap, c + 1, c)

    for chains in [range(g, g + CHAINS_PER_STEP) for g in range(0, n_chains, CHAINS_PER_STEP)]:
        def diag_scores(slot, i, chains=chains):
            for ch in chains:
                hd = ch // n_maps
                qt = qt_ref[i, ch]
                s = jnp.dot(k_ref[0, rows(N_META + i * tq), head_cols(hd)], qt, preferred_element_type=F32)
                s = jnp.where(causal, s, -jnp.inf)
                sm = jnp.dot(k_ref[0, 0:N_META, head_cols(hd)], qt, preferred_element_type=F32)
                s_ref[slot, ch] = s
                sm_ref[slot, ch] = sm
                cmax_ref[slot, ch] = jnp.maximum(jnp.max(s, axis=0, keepdims=True),
                                                 jnp.max(sm, axis=0, keepdims=True))

        def diag_softmax(slot, i, chains=chains):
            for ch in chains:
                m = cmax_ref[slot, ch]
                p = jnp.exp2(s_ref[slot, ch] - m)
                pm = jnp.exp2(sm_ref[slot, ch] - m)
                p_ref[slot, ch] = p.astype(BF16)
                pm_ref[slot, ch] = pm.astype(BF16)
                m_ref[i, ch] = m

        def diag_values(slot, i, chains=chains):
            for ch in chains:
                hd = ch // n_maps
                acc_ref[i, ch] = (jnp.dot(vt_ref[i, hd], p_ref[slot, ch], preferred_element_type=F32)
                                  + jnp.dot(vtm_ref[hd], pm_ref[slot, ch], preferred_element_type=F32))

        _software_pipeline(n_tiles, lambda s: (s,), lambda it: (it[0] + 1,),
                           diag_scores, diag_softmax, diag_values)

        def full_scores(slot, i, c, chains=chains):
            for ch in chains:
                hd = ch // n_maps
                s = jnp.dot(k_ref[0, rows(N_META + c * tq), head_cols(hd)], qt_ref[i, ch],
                            preferred_element_type=F32)
                s_ref[slot, ch] = s
                cmax_ref[slot, ch] = jnp.max(s, axis=0, keepdims=True)

        def full_softmax(slot, i, c, chains=chains):
            for ch in chains:
                m_old = m_ref[i, ch]
                m_new = jnp.maximum(m_old, cmax_ref[slot, ch])
                alpha = jnp.exp2(m_old - m_new)
                p = jnp.exp2(s_ref[slot, ch] - m_new)
                p_ref[slot, ch] = p.astype(BF16)
                alpha_ref[slot, ch] = alpha
                m_ref[i, ch] = m_new

        def full_values(slot, i, c, chains=chains):
            for ch in chains:
                hd = ch // n_maps
                pv = jnp.dot(vt_ref[c, hd], p_ref[slot, ch], preferred_element_type=F32)
                acc_ref[i, ch] = alpha_ref[slot, ch] * acc_ref[i, ch] + pv

        _software_pipeline(len(pairs), lambda s: pairs[s], next_pair, full_scores, full_softmax, full_values)

    def finalize(i, _):
        o_list = [[acc_ref[i, hd * n_maps + mp, 0:V_DIM] * (1.0 / acc_ref[i, hd * n_maps + mp, V_DIM:V_DIM + 1])
                   for mp in range(n_maps)] for hd in range(HEADS)]
        for hd, o in enumerate(finish(o_list)):
            o_ref[0, rows(N_META + i * tq), v_cols(hd)] = o
        return 0

    lax.fori_loop(0, n_tiles, finalize, 0)


def _attention(q, k, v, *, n_maps, lam_init=0.0, extras=()):
    b, seq_len, wq = q.shape
    n_tiles = (seq_len - N_META) // ATT_TILE
    assert N_META + n_tiles * ATT_TILE == seq_len
    n_chains = HEADS * n_maps
    hw = wq // HEADS
    tq = ATT_TILE
    seq = lambda w: pl.BlockSpec((1, seq_len, w), lambda i: (i, 0, 0))
    kern = functools.partial(_attention_kernel, n_maps=n_maps, lam_init=lam_init, n_tiles=n_tiles)
    scratch = [
        pltpu.VMEM((n_tiles, HEADS, VT_ROWS, tq), BF16),
        pltpu.VMEM((HEADS, VT_ROWS, N_META), BF16),
        pltpu.VMEM((n_tiles, n_chains, hw, tq), BF16),
        pltpu.VMEM((2, n_chains, tq, tq), F32),
        pltpu.VMEM((2, n_chains, N_META, tq), F32),
        pltpu.VMEM((2, n_chains, 1, tq), F32),
        pltpu.VMEM((2, n_chains, tq, tq), BF16),
        pltpu.VMEM((2, n_chains, N_META, tq), BF16),
        pltpu.VMEM((2, n_chains, 1, tq), F32),
        pltpu.VMEM((n_tiles, n_chains, VT_ROWS, tq), F32),
        pltpu.VMEM((n_tiles, n_chains, 1, tq), F32),
    ]
    return pl.pallas_call(
        kern,
        grid=(b,),
        in_specs=[seq(wq), seq(wq), seq(HEADS * V_DIM)] + [_const_spec(e.shape) for e in extras],
        out_specs=seq(HEADS * V_DIM),
        out_shape=jax.ShapeDtypeStruct((b, seq_len, HEADS * V_DIM), BF16),
        scratch_shapes=scratch,
        compiler_params=pltpu.CompilerParams(dimension_semantics=("arbitrary",),
                                             vmem_limit_bytes=VMEM_LIMIT_BYTES),
        name="attn_diff" if n_maps == 2 else "attn_mla",
    )(q, k, v, *extras)


def _out_ffn_kernel(h_ref, oa_ref, ob_ref, wo_ref, g_ref, wgu_ref, wd_ref, out_ref, *, d_ff):
    half = oa_ref.shape[1]
    h = (h_ref[...]
         + jnp.dot(oa_ref[...], wo_ref[0:half, :], preferred_element_type=F32)
         + jnp.dot(ob_ref[...], wo_ref[half:2 * half, :], preferred_element_type=F32))
    hn = (h * _rms_scale(h) * g_ref[...]).astype(BF16)
    g = jnp.dot(hn, wgu_ref[:, 0:d_ff], preferred_element_type=F32)
    u = jnp.dot(hn, wgu_ref[:, d_ff:2 * d_ff], preferred_element_type=F32)
    a = (g * jax.nn.sigmoid(g) * u).astype(BF16)
    out_ref[...] = h + jnp.dot(a, wd_ref[...], preferred_element_type=F32)


def _out_ffn(h, oa, ob, lp, *, row_tile):
    t, d = h.shape
    d_ff = lp['w_d'].shape[0]
    row = lambda w: pl.BlockSpec((row_tile, w), lambda i: (i, 0))
    consts = [lp['w_o'], lp['g_ffn'], lp['w_gu'], lp['w_d']]
    kern = functools.partial(_out_ffn_kernel, d_ff=d_ff)
    return pl.pallas_call(
        kern,
        grid=(t // row_tile,),
        in_specs=[row(d), row(oa.shape[1]), row(ob.shape[1])] + [_const_spec(c.shape) for c in consts],
        out_specs=row(d),
        out_shape=jax.ShapeDtypeStruct((t, d), F32),
        compiler_params=pltpu.CompilerParams(dimension_semantics=("arbitrary",),
                                             vmem_limit_bytes=VMEM_LIMIT_BYTES),
        name="out_ffn",
    )(h, oa, ob, *consts)


def _out_ffn_drop_meta(h, oa, ob, lp, *, batch, seq_len):
    t, d = h.shape
    s = seq_len - N_META
    tile = _row_tile(s)
    d_ff = lp['w_d'].shape[0]
    seq3 = lambda a: a.reshape(batch, seq_len, a.shape[1])
    window = lambda w: pl.BlockSpec((pl.Element(tile), pl.Element(w)),
                                    lambda bi, j: (pl.multiple_of(
                                        (bi * (seq_len // BF16_ROWS) + N_META // BF16_ROWS
                                         + (tile // BF16_ROWS) * j) * BF16_ROWS, BF16_ROWS), 0))
    consts = [lp['w_o'], lp['g_ffn'], lp['w_gu'], lp['w_d']]

    def kern(h_ref, oa_ref, ob_ref, wo_ref, g_ref, wgu_ref, wd_ref, out_ref):
        _out_ffn_kernel(h_ref, oa_ref, ob_ref, wo_ref, g_ref, wgu_ref, wd_ref, out_ref.at[0], d_ff=d_ff)

    return pl.pallas_call(
        kern,
        grid=(batch, s // tile),
        in_specs=[window(d), window(oa.shape[1]), window(ob.shape[1])] + [_const_spec(c.shape) for c in consts],
        out_specs=pl.BlockSpec((1, tile, d), lambda bi, j: (bi, j, 0)),
        out_shape=jax.ShapeDtypeStruct((batch, s, d), F32),
        compiler_params=pltpu.CompilerParams(dimension_semantics=("arbitrary", "arbitrary"),
                                             vmem_limit_bytes=VMEM_LIMIT_BYTES),
        name="out_ffn_last",
    )(h, oa, ob, *consts)


def _rope_tables(n):
    pos = jnp.arange(n, dtype=F32)
    inv = 1.0 / (ROPE_THETA ** (jnp.arange(0, ROPE, 2, dtype=F32) / ROPE))
    ang = pos[:, None] * inv[None, :]
    emb = jnp.concatenate([ang, ang], axis=-1)
    cos, sin = jnp.cos(emb), jnp.sin(emb)
    first = jnp.arange(ROPE) < ROPE // 2
    sin_lo = jnp.where(first, -sin, 0.0)
    sin_hi = jnp.where(first, 0.0, sin)
    zeros = jnp.zeros_like(cos)
    pad = lambda a: jnp.concatenate([a, zeros], axis=-1)
    dup = lambda a: jnp.concatenate([a, a], axis=-1)
    return (pad(cos), pad(sin_lo), pad(sin_hi), dup(cos), dup(sin_lo), dup(sin_hi))


def _layer_params(l, attn_norm, w_in, mla_q_a_norm, w_q_up, mla_kv_a_norm, w_kv_up, mla_q_norm,
                  mla_k_norm, diff_q_norm, diff_k_norm, diff_subln, w_o, ffn_norm, w_gate_up, w_down):
    d = w_in.shape[1]
    q_rank, kv_rank = mla_q_a_norm.shape[1], mla_kv_a_norm.shape[1]
    wi = w_in[l]
    o_kr = q_rank + kv_rank
    w_in_p = jnp.concatenate([wi[:, :o_kr + ROPE], jnp.zeros((d, LANES - ROPE), F32), wi[:, o_kr + ROPE:]], axis=1)
    wq = w_q_up[l].reshape(q_rank, HEADS, QK_A)
    wq = jnp.concatenate([wq, jnp.zeros((q_rank, HEADS, HEAD_PAD - QK_A), F32)], axis=-1)
    zpad = jnp.zeros((HEAD_PAD - QK_A,), F32)
    row2 = lambda a: a.reshape(1, -1)
    return dict(
        g_attn=row2(attn_norm[l]), w_in=w_in_p.astype(BF16),
        g_qa=row2(mla_q_a_norm[l]), w_qup=wq.reshape(q_rank, HEADS * HEAD_PAD).astype(BF16),
        g_kva=row2(mla_kv_a_norm[l]), w_kvup=w_kv_up[l].astype(BF16),
        g_q=row2(jnp.concatenate([mla_q_norm[l], zpad])),
        g_kn=row2(mla_k_norm[l][:NOPE]),
        g_kr=row2(jnp.concatenate([mla_k_norm[l][NOPE:], zpad])),
        g_dq=row2(jnp.concatenate([diff_q_norm[l]] * 2)), g_dk=row2(jnp.concatenate([diff_k_norm[l]] * 2)),
        g_sub=diff_subln[l].reshape(-1, 1),
        w_o=w_o[l].astype(BF16), g_ffn=row2(ffn_norm[l]),
        w_gu=w_gate_up[l].astype(BF16), w_d=w_down[l].astype(BF16),
    )


def _row_tile(seq_len):
    best = BF16_ROWS
    for cand in range(BF16_ROWS, ROW_TILE_MAX + 1, BF16_ROWS):
        if seq_len % cand == 0:
            best = cand
    return best


def kernel(x, meta_tokens, attn_norm, w_in, mla_q_a_norm, w_q_up, mla_kv_a_norm, w_kv_up, mla_q_norm,
           mla_k_norm, diff_q_norm, diff_k_norm, lambda_q1, lambda_k1, lambda_q2, lambda_k2, diff_subln,
           w_o, ffn_norm, w_gate_up, w_down):
    b, s, d = x.shape
    depth = w_in.shape[0]
    seq_len = N_META + s
    meta = jnp.broadcast_to(meta_tokens[None].astype(x.dtype), (b, N_META, d))
    h = jnp.concatenate([meta, x], axis=1).reshape(b * seq_len, d)
    row_tile = _row_tile(seq_len)
    tabs = _rope_tables(seq_len)

    for l in range(depth):
        lp = _layer_params(l, attn_norm, w_in, mla_q_a_norm, w_q_up, mla_kv_a_norm, w_kv_up, mla_q_norm,
                           mla_k_norm, diff_q_norm, diff_k_norm, diff_subln, w_o, ffn_norm, w_gate_up, w_down)
        qa, ka, va, qd, kd, vd = _inproj(h, lp, tabs, seq_len=seq_len, row_tile=row_tile)
        seq3 = lambda a: a.reshape(b, seq_len, a.shape[1])
        oa = _attention(seq3(qa), seq3(ka), seq3(va), n_maps=1)
        lam_init = 0.8 - 0.6 * math.exp(-0.3 * l)
        lams = [a[l].reshape(1, -1) for a in (lambda_q1, lambda_k1, lambda_q2, lambda_k2)]
        ob = _attention(seq3(qd), seq3(kd), seq3(vd), n_maps=2, lam_init=lam_init, extras=lams + [lp['g_sub']])
        oa, ob = oa.reshape(b * seq_len, -1), ob.reshape(b * seq_len, -1)
        if l == depth - 1:
            return _out_ffn_drop_meta(h, oa, ob, lp, batch=b, seq_len=seq_len)
        h = _out_ffn(h, oa, ob, lp, row_tile=row_tile)
```
